```python
import jax, jax.numpy as jnp
from jax import lax
import numpy as np

D_MODEL = 1024
BATCH = 4
SEQ = 4096
DEPTH = 2
DEC_BATCH = 32
DEC_SEQ = 64
PAST_LEN = 4096

CHUNK = 64
N_HEADS = 8
HEAD_DIM = 64
N_KV_HEADS = 2
KV_GROUP = N_HEADS // N_KV_HEADS
ROPE_DIM = HEAD_DIM // 4
ROPE_THETA = 500000.0
N_IDX_HEADS = 8
IDX_DIM = 64
IDX_ROPE_DIM = IDX_DIM // 4
TOPK_MAX = 256
QUERY_BLOCK = 128
D_ATTN = N_HEADS * HEAD_DIM
D_CONV = 512
CONV_W = 31
MIX_WIDTH = D_ATTN + D_CONV
N_GROUPS = 4
EXPERTS_PER_GROUP = 4
N_EXPERTS = N_GROUPS * EXPERTS_PER_GROUP
TOP_K_EXPERTS = 2
D_EXPERT = 256
COLS_Q = D_ATTN
COLS_K = N_KV_HEADS * HEAD_DIM
COLS_V = N_KV_HEADS * HEAD_DIM
COLS_IQ = N_IDX_HEADS * IDX_DIM
COLS_IK = IDX_DIM
COLS_IW = N_IDX_HEADS
COLS_CONV = 2 * D_CONV
IN_COLS = COLS_Q + COLS_K + COLS_V + COLS_IQ + COLS_IK + COLS_IW + COLS_CONV
SPLIT_POINTS = (COLS_Q, COLS_Q + COLS_K, COLS_Q + COLS_K + COLS_V,
                COLS_Q + COLS_K + COLS_V + COLS_IQ,
                COLS_Q + COLS_K + COLS_V + COLS_IQ + COLS_IK,
                COLS_Q + COLS_K + COLS_V + COLS_IQ + COLS_IK + COLS_IW)
EPS = 1e-6

kernel_name = 'hymba_dsa_conformer_hmoe_stream_step'


def rms_norm(x, g):
    xf = x.astype(jnp.float32)
    y = xf * lax.rsqrt(jnp.mean(xf * xf, axis=-1, keepdims=True) + EPS)
    return (y * g.astype(jnp.float32)).astype(x.dtype)


def layer_norm(x, g, b):
    xf = x.astype(jnp.float32)
    mu = jnp.mean(xf, axis=-1, keepdims=True)
    xc = xf - mu
    var = jnp.mean(xc * xc, axis=-1, keepdims=True)
    y = xc * lax.rsqrt(var + EPS) * g.astype(jnp.float32) + b.astype(jnp.float32)
    return y.astype(x.dtype)


def partial_rope(x, pos, rot_dim):
    half = rot_dim // 2
    inv = ROPE_THETA ** (-jnp.arange(half, dtype=jnp.float32) / half)
    ang = pos.astype(jnp.float32)[:, None] * inv[None, :]
    cos = jnp.cos(ang)[:, None, :]
    sin = jnp.sin(ang)[:, None, :]
    xf = x.astype(jnp.float32)
    x1 = xf[..., :half]
    x2 = xf[..., half:rot_dim]
    out = jnp.concatenate([x1 * cos - x2 * sin, x2 * cos + x1 * sin, xf[..., rot_dim:]], axis=-1)
    return out.astype(x.dtype)


def dsa_query_block(blk, k_all, v_all, ik_all, k_chunk, k_top):
    q, qi, wi, q_chunk = blk
    B, QB = q.shape[0], q.shape[1]
    S = k_all.shape[1]
    dots = jnp.einsum('bqhd,bsd->bqhs', qi.astype(jnp.float32), ik_all)
    score = jnp.einsum('bqhs,bqh->bqs', jax.nn.relu(dots), wi.astype(jnp.float32))
    admissible = k_chunk[None, :] <= q_chunk[:, None]
    score = jnp.where(admissible[None], score, -jnp.inf)
    top_val, top_idx = lax.top_k(score, k_top)
    valid = jnp.isfinite(top_val)
    flat = top_idx.reshape(B, QB * k_top)
    bidx = jnp.arange(B)[:, None]
    k_sel = k_all.reshape(B, S, COLS_K)[bidx, flat].reshape(B, QB, k_top, N_KV_HEADS, HEAD_DIM)
    v_sel = v_all.reshape(B, S, COLS_V)[bidx, flat].reshape(B, QB, k_top, N_KV_HEADS, HEAD_DIM)
    qg = q.astype(jnp.float32).reshape(B, QB, N_KV_HEADS, KV_GROUP, HEAD_DIM)
    logits = jnp.einsum('bqngd,bqknd->bqngk', qg, k_sel) * (HEAD_DIM ** -0.5)
    logits = jnp.where(valid[:, :, None, None, :], logits, -jnp.inf)
    p = jax.nn.softmax(logits, axis=-1)
    o = jnp.einsum('bqngk,bqknd->bqngd', p, v_sel)
    return o.reshape(B, QB, D_ATTN).astype(q.dtype)


def dsa_attention(q, qi, wi, q_pos, k_all, v_all, ik_all, k_pos):
    B, T = q.shape[0], q.shape[1]
    S = k_all.shape[1]
    k_top = min(TOPK_MAX, S // 4)
    qb = QUERY_BLOCK if T % QUERY_BLOCK == 0 else T
    nb = T // qb

    def to_blocks(a):
        return jnp.moveaxis(a.reshape((B, nb, qb) + a.shape[2:]), 1, 0)

    k_f = k_all.astype(jnp.float32)
    v_f = v_all.astype(jnp.float32)
    ik_f = ik_all.astype(jnp.float32)
    k_chunk = k_pos // CHUNK
    q_chunk = (q_pos // CHUNK).reshape(nb, qb)
    out = lax.map(lambda blk: dsa_query_block(blk, k_f, v_f, ik_f, k_chunk, k_top),
                  (to_blocks(q), to_blocks(qi), to_blocks(wi), q_chunk))
    return jnp.moveaxis(out, 0, 1).reshape(B, T, D_ATTN)


def hybrid_mixer(h, pos, past, w_in, q_norm_g, k_norm_g, conv_w, conv_b, ln_conv_g, ln_conv_b):
    B, T = h.shape[0], h.shape[1]
    proj = h @ w_in
    q, k, v, iq, ik, iw, u = jnp.split(proj, SPLIT_POINTS, axis=-1)
    q = partial_rope(rms_norm(q.reshape(B, T, N_HEADS, HEAD_DIM), q_norm_g), pos, ROPE_DIM)
    k = partial_rope(rms_norm(k.reshape(B, T, N_KV_HEADS, HEAD_DIM), k_norm_g), pos, ROPE_DIM)
    v = v.reshape(B, T, N_KV_HEADS, HEAD_DIM)
    iq = partial_rope(iq.reshape(B, T, N_IDX_HEADS, IDX_DIM), pos, IDX_ROPE_DIM)
    ik = partial_rope(ik[:, :, None, :], pos, IDX_ROPE_DIM)[:, :, 0, :]
    iw = iw * (N_IDX_HEADS ** -0.5 * IDX_DIM ** -0.5)
    a, b = jnp.split(u, 2, axis=-1)
    g = a * jax.nn.sigmoid(b)
    if past is None:
        k_all, v_all, ik_all, k_pos = k, v, ik, pos
        conv_past = jnp.zeros((B, CONV_W - 1, D_CONV), g.dtype)
    else:
        k_past, v_past, ik_past, conv_past = past
        past_len = k_past.shape[1]
        k_all = jnp.concatenate([k_past.astype(k.dtype), k], axis=1)
        v_all = jnp.concatenate([v_past.astype(v.dtype), v], axis=1)
        ik_all = jnp.concatenate([ik_past.astype(ik.dtype), ik], axis=1)
        k_pos = jnp.concatenate([jnp.arange(past_len, dtype=jnp.int32), pos])
        conv_past = conv_past.astype(g.dtype)
    attn = dsa_attention(q, iq, iw, pos, k_all, v_all, ik_all, k_pos)
    padded = jnp.concatenate([conv_past, g], axis=1)
    y = lax.conv_general_dilated(padded, conv_w.astype(g.dtype)[:, None, :], window_strides=(1,),
                                 padding='VALID', dimension_numbers=('NWC', 'WIO', 'NWC'),
                                 feature_group_count=D_CONV) + conv_b
    y = jax.nn.silu(layer_norm(y, ln_conv_g, ln_conv_b))
    mix = jnp.concatenate([attn, y], axis=-1)
    return mix, (k, v, ik, padded[:, -(CONV_W - 1):])


def hierarchical_moe(h, w_router_group, b_router_group, w_router_expert, b_router_expert,
                     w_gate, w_up, w_down):
    B, T, D = h.shape
    hf = h.reshape(B * T, D)
    n = hf.shape[0]
    g_logits = (hf @ w_router_group + b_router_group).astype(jnp.float32)
    p_group = jax.nn.softmax(g_logits, axis=-1)
    g_sel = jnp.argmax(g_logits, axis=-1)
    p_g = jnp.take_along_axis(p_group, g_sel[:, None], axis=1)
    e_logits = (hf @ w_router_expert + b_router_expert).astype(jnp.float32)
    e_logits = e_logits.reshape(n, N_GROUPS, EXPERTS_PER_GROUP)
    e_in_group = jnp.take_along_axis(e_logits, g_sel[:, None, None], axis=1)[:, 0]
    p_e = jax.nn.softmax(e_in_group, axis=-1)
    top_p, top_i = lax.top_k(p_e, TOP_K_EXPERTS)
    top_p = top_p / jnp.sum(top_p, axis=-1, keepdims=True)
    weights = p_g * top_p
    expert_id = g_sel[:, None] * EXPERTS_PER_GROUP + top_i
    gate = jnp.sum(jax.nn.one_hot(expert_id, N_EXPERTS, dtype=jnp.float32) * weights[..., None], axis=1)
    hid = jax.nn.silu(jnp.einsum('nd,edf->nef', hf, w_gate)) * jnp.einsum('nd,edf->nef', hf, w_up)
    hid = hid * gate[:, :, None].astype(hid.dtype)
    out = jnp.einsum('nef,efd->nd', hid, w_down)
    return out.reshape(B, T, D)


def trunk_layer(x, c, pos, past, w_ada, b_ada, g_norm_mix, w_in, q_norm_g, k_norm_g, conv_w, conv_b,
                ln_conv_g, ln_conv_b, w_out, g_norm_ffn, w_router_group, b_router_group,
                w_router_expert, b_router_expert, w_gate, w_up, w_down):
    mod = (jax.nn.silu(c) @ w_ada + b_ada)[:, None, :]
    sh_a, sc_a, gt_a, sh_f, sc_f, gt_f = jnp.split(mod, 6, axis=-1)
    h = rms_norm(x, g_norm_mix) * (1 + sc_a) + sh_a
    mix, new_state = hybrid_mixer(h, pos, past, w_in, q_norm_g, k_norm_g, conv_w, conv_b,
                                  ln_conv_g, ln_conv_b)
    x = x + gt_a * (mix @ w_out)
    h = rms_norm(x, g_norm_ffn) * (1 + sc_f) + sh_f
    x = x + gt_f * hierarchical_moe(h, w_router_group, b_router_group, w_router_expert,
                                    b_router_expert, w_gate, w_up, w_down)
    return x, new_state


def setup_inputs(seed: int = 0) -> dict:
    key = jax.random.key(seed)
    ks = jax.random.split(key, 32)
    f32 = jnp.float32

    def nrm(k, shape, scale):
        return jax.random.normal(k, shape, f32) * scale

    return {
        'x_prompt': nrm(ks[0], (BATCH, SEQ, D_MODEL), 1.0),
        'x_sample': nrm(ks[1], (DEC_BATCH, DEC_SEQ, D_MODEL), 1.0),
        'c_prompt': nrm(ks[2], (BATCH, D_MODEL), 1.0),
        'c_sample': nrm(ks[3], (DEC_BATCH, D_MODEL), 1.0),
        'cache_k': nrm(ks[4], (DEPTH, DEC_BATCH, PAST_LEN, N_KV_HEADS, HEAD_DIM), 1.0),
        'cache_v': nrm(ks[5], (DEPTH, DEC_BATCH, PAST_LEN, N_KV_HEADS, HEAD_DIM), 1.0),
        'cache_idx_k': nrm(ks[6], (DEPTH, DEC_BATCH, PAST_LEN, IDX_DIM), 1.0),
        'state_conv': nrm(ks[7], (DEPTH, DEC_BATCH, CONV_W - 1, D_CONV), 0.5),
        'w_ada': nrm(ks[8], (DEPTH, D_MODEL, 6 * D_MODEL), 0.5 * D_MODEL ** -0.5),
        'b_ada': nrm(ks[9], (DEPTH, 6 * D_MODEL), 0.01),
        'g_norm_mix': 1.0 + nrm(ks[10], (DEPTH, D_MODEL), 0.01),
        'w_in': nrm(ks[11], (DEPTH, D_MODEL, IN_COLS), D_MODEL ** -0.5),
        'q_norm_g': 1.0 + nrm(ks[12], (DEPTH, HEAD_DIM), 0.01),
        'k_norm_g': 1.0 + nrm(ks[13], (DEPTH, HEAD_DIM), 0.01),
        'conv_w': nrm(ks[14], (DEPTH, CONV_W, D_CONV), CONV_W ** -0.5),
        'conv_b': nrm(ks[15], (DEPTH, D_CONV), 0.01),
        'ln_conv_g': 1.0 + nrm(ks[16], (DEPTH, D_CONV), 0.01),
        'ln_conv_b': nrm(ks[17], (DEPTH, D_CONV), 0.01),
        'w_out': nrm(ks[18], (DEPTH, MIX_WIDTH, D_MODEL), MIX_WIDTH ** -0.5),
        'g_norm_ffn': 1.0 + nrm(ks[19], (DEPTH, D_MODEL), 0.01),
        'w_router_group': nrm(ks[20], (DEPTH, D_MODEL, N_GROUPS), D_MODEL ** -0.5),
        'b_router_group': nrm(ks[21], (DEPTH, N_GROUPS), 0.01),
        'w_router_expert': nrm(ks[22], (DEPTH, D_MODEL, N_EXPERTS), D_MODEL ** -0.5),
        'b_router_expert': nrm(ks[23], (DEPTH, N_EXPERTS), 0.01),
        'w_gate': nrm(ks[24], (DEPTH, N_EXPERTS, D_MODEL, D_EXPERT), D_MODEL ** -0.5),
        'w_up': nrm(ks[25], (DEPTH, N_EXPERTS, D_MODEL, D_EXPERT), D_MODEL ** -0.5),
        'w_down': nrm(ks[26], (DEPTH, N_EXPERTS, D_EXPERT, D_MODEL), D_EXPERT ** -0.5),
    }


def reference(x_prompt, x_sample, c_prompt, c_sample, cache_k, cache_v, cache_idx_k, state_conv,
              w_ada, b_ada, g_norm_mix, w_in, q_norm_g, k_norm_g, conv_w, conv_b, ln_conv_g,
              ln_conv_b, w_out, g_norm_ffn, w_router_group, b_router_group, w_router_expert,
              b_router_expert, w_gate, w_up, w_down):
    def params_at(l):
        return (w_ada[l], b_ada[l], g_norm_mix[l], w_in[l], q_norm_g[l], k_norm_g[l], conv_w[l],
                conv_b[l], ln_conv_g[l], ln_conv_b[l], w_out[l], g_norm_ffn[l], w_router_group[l],
                b_router_group[l], w_router_expert[l], b_router_expert[l], w_gate[l], w_up[l],
                w_down[l])

    x = x_prompt
    pos_p = jnp.arange(x_prompt.shape[1], dtype=jnp.int32)
    st_p = []
    for l in range(DEPTH):
        x, st = trunk_layer(x, c_prompt, pos_p, None, *params_at(l))
        st_p.append(st)
    y_prompt = x

    past_len = cache_k.shape[2]
    x = x_sample
    pos_s = past_len + jnp.arange(x_sample.shape[1], dtype=jnp.int32)
    st_s = []
    for l in range(DEPTH):
        x, st = trunk_layer(x, c_sample, pos_s, (cache_k[l], cache_v[l], cache_idx_k[l], state_conv[l]),
                            *params_at(l))
        st_s.append(st)
    y_sample = x

    new_k_prompt = jnp.stack([s[0] for s in st_p])
    new_v_prompt = jnp.stack([s[1] for s in st_p])
    new_idx_k_prompt = jnp.stack([s[2] for s in st_p])
    new_conv_prompt = jnp.stack([s[3] for s in st_p])
    new_k_sample = jnp.stack([s[0] for s in st_s])
    new_v_sample = jnp.stack([s[1] for s in st_s])
    new_idx_k_sample = jnp.stack([s[2] for s in st_s])
    new_conv_sample = jnp.stack([s[3] for s in st_s])
    return (y_prompt, y_sample, new_k_prompt, new_v_prompt, new_idx_k_prompt, new_conv_prompt,
            new_k_sample, new_v_sample, new_idx_k_sample, new_conv_sample)
```

```python
import functools

import jax
import jax.numpy as jnp
import numpy as np
from jax import lax
from jax.experimental import pallas as pl
from jax.experimental.pallas import tpu as pltpu

D_MODEL = 1024
CHUNK = 64
N_HEADS = 8
HEAD_DIM = 64
N_KV_HEADS = 2
KV_GROUP = N_HEADS // N_KV_HEADS
ROPE_DIM = HEAD_DIM // 4
ROPE_THETA = 500000.0
N_IDX_HEADS = 8
IDX_DIM = 64
TOPK_MAX = 256
D_ATTN = N_HEADS * HEAD_DIM
D_CONV = 512
CONV_W = 31
N_GROUPS = 4
EXPERTS_PER_GROUP = 4
N_EXPERTS = N_GROUPS * EXPERTS_PER_GROUP
D_EXPERT = 256
EPS = 1e-6

LANES = 128
SUBLANES = 8
KEY_TILE = 256
NEG_BIAS = -1e30
INT_MIN = -(2 ** 31)
VMEM_LIMIT = 48 * 1024 * 1024

COL_Q = 0
COL_K = 512
COL_V = 640
COL_IQ = 768
COL_IKW = 1280
COL_A = 1408
COL_B = 1920
W_CAT_COLS = 2432
IN_SPLIT = 1352

F32 = jnp.float32
BF16 = jnp.bfloat16
NT_DIMS = (((1,), (1,)), ((), ()))


def _cparams(sem):
    return pltpu.CompilerParams(dimension_semantics=sem, vmem_limit_bytes=VMEM_LIMIT)


def _ada_kernel(c_ref, w_ref, b_ref, o_ref):
    c = c_ref[...]
    s = (c * jax.nn.sigmoid(c)).astype(BF16)
    o_ref[0] = jnp.dot(s, w_ref[0].astype(BF16), preferred_element_type=F32) + b_ref[0]


def _ada_call(c_all, w_ada, b_ada):
    depth, d, n6 = w_ada.shape
    bp = c_all.shape[0]
    tn = 1536
    return pl.pallas_call(
        _ada_kernel,
        grid=(depth, n6 // tn),
        in_specs=[
            pl.BlockSpec((bp, d), lambda l, j: (0, 0)),
            pl.BlockSpec((1, d, tn), lambda l, j: (l, 0, j)),
            pl.BlockSpec((1, 1, tn), lambda l, j: (l, 0, j)),
        ],
        out_specs=pl.BlockSpec((1, bp, tn), lambda l, j: (l, 0, j)),
        out_shape=jax.ShapeDtypeStruct((depth, bp, n6), F32),
        compiler_params=_cparams(("arbitrary", "arbitrary")),
        name="ada_mod",
    )(c_all, w_ada, b_ada.reshape(depth, 1, n6))


def _head_mean_sq(y, bd):
    sq = y * y
    hi = sq.astype(BF16)
    lo = (sq - hi.astype(F32)).astype(BF16)
    s = jnp.dot(hi, bd, preferred_element_type=F32) + jnp.dot(lo, bd, preferred_element_type=F32)
    return s * (1.0 / HEAD_DIM)


def _rope(y, c, sa, sb):
    up = pltpu.roll(y, LANES - ROPE_DIM // 2, 1)
    dn = pltpu.roll(y, ROPE_DIM // 2, 1)
    return y * c + up * sa + dn * sb


def _inproj_kernel(x_ref, sh_ref, sc_ref, gn_ref, w_ref, gqk_ref, tabs_ref, bd_ref,
                   q_ref, iq_ref, kf_ref, vf_ref, kb_ref, va_ref, ikwf_ref, ikwb_ref, g_ref,
                   *, bb, rt):
    m = bb * rt
    x = x_ref[...]
    ms = jnp.mean(x * x, axis=-1, keepdims=True)
    h = x * lax.rsqrt(ms + EPS) * gn_ref[...]
    h = h * (1.0 + sc_ref[...]) + sh_ref[...]
    hb = h.reshape(m, D_MODEL).astype(BF16)

    def tab(i):
        t = tabs_ref[i]
        if bb > 1:
            t = jnp.broadcast_to(t[None], (bb, rt, LANES)).reshape(m, LANES)
        return t

    cq, saq, sbq, ci, sai, sbi = (tab(i) for i in range(6))
    bd = bd_ref[...]
    gq = gqk_ref[0:1, :]
    gk = gqk_ref[1:2, :]

    def proj(c0, n):
        return jnp.dot(hb, w_ref[:, c0:c0 + n], preferred_element_type=F32)

    def put(ref, c0, val):
        ref[:, :, c0:c0 + val.shape[-1]] = val.reshape(bb, rt, val.shape[-1]).astype(ref.dtype)

    pq = proj(COL_Q, D_ATTN)
    for j in range(D_ATTN // LANES):
        y = pq[:, j * LANES:(j + 1) * LANES]
        y = y * lax.rsqrt(_head_mean_sq(y, bd) + EPS) * gq
        put(q_ref, j * LANES, _rope(y, cq, saq, sbq) * (HEAD_DIM ** -0.5))

    pkv = proj(COL_K, 2 * LANES)
    y = pkv[:, :LANES]
    y = y * lax.rsqrt(_head_mean_sq(y, bd) + EPS) * gk
    kk = _rope(y, cq, saq, sbq)
    put(kf_ref, 0, kk)
    put(kb_ref, 0, kk)
    vv = pkv[:, LANES:]
    put(vf_ref, 0, vv)
    lane = lax.broadcasted_iota(jnp.int32, (m, LANES), 1)
    put(va_ref, 0, jnp.where(lane < HEAD_DIM, vv, 1.0))
    put(va_ref, LANES, jnp.where(lane < HEAD_DIM, pltpu.roll(vv, HEAD_DIM, 1), 1.0))

    piq = proj(COL_IQ, N_IDX_HEADS * IDX_DIM)
    for j in range(N_IDX_HEADS * IDX_DIM // LANES):
        put(iq_ref, j * LANES, _rope(piq[:, j * LANES:(j + 1) * LANES], cq, saq, sbq))

    ikw = _rope(proj(COL_IKW, LANES), ci, sai, sbi)
    put(ikwf_ref, 0, ikw)
    put(ikwb_ref, 0, ikw)

    pu = proj(COL_A, 2 * D_CONV)
    put(g_ref, 0, pu[:, :D_CONV] * jax.nn.sigmoid(pu[:, D_CONV:]))


def _inproj_call(x, mod, gn, w_cat, gqk, tabs, bd, *, bb, rt, tab_blocked):
    nb, r, d = x.shape
    grid = (nb // bb, r // rt)
    tok = lambda n, dt: jax.ShapeDtypeStruct((nb, r, n), dt)
    tspec = lambda n: pl.BlockSpec((bb, rt, n), lambda i, j: (i, j, 0))
    return pl.pallas_call(
        functools.partial(_inproj_kernel, bb=bb, rt=rt),
        grid=grid,
        in_specs=[
            tspec(d),
            pl.BlockSpec((bb, 1, d), lambda i, j: (i, 0, 0)),
            pl.BlockSpec((bb, 1, d), lambda i, j: (i, 0, 1)),
            pl.BlockSpec((1, d), lambda i, j: (0, 0)),
            pl.BlockSpec((d, W_CAT_COLS), lambda i, j: (0, 0)),
            pl.BlockSpec((2, LANES), lambda i, j: (0, 0)),
            pl.BlockSpec((6, rt, LANES), (lambda i, j: (0, j, 0)) if tab_blocked else (lambda i, j: (0, 0, 0))),
            pl.BlockSpec((LANES, LANES), lambda i, j: (0, 0)),
        ],
        out_specs=[tspec(D_ATTN), tspec(D_ATTN), tspec(LANES), tspec(LANES), tspec(LANES),
                   tspec(2 * LANES), tspec(LANES), tspec(LANES), tspec(D_CONV)],
        out_shape=[tok(D_ATTN, BF16), tok(D_ATTN, BF16), tok(LANES, F32), tok(LANES, F32),
                   tok(LANES, BF16), tok(2 * LANES, BF16), tok(LANES, F32), tok(LANES, BF16),
                   tok(D_CONV, F32)],
        compiler_params=_cparams(("arbitrary", "arbitrary")),
        name="in_proj",
    )(x, mod, mod, gn, w_cat, gqk, tabs, bd)


def _stack_heads(ref, tq, heads):
    lane = lax.broadcasted_iota(jnp.int32, (tq, LANES), 1)
    low = lane < HEAD_DIM
    out = []
    for h, half in heads:
        slab = ref[0, :, (h // 2) * LANES:(h // 2 + 1) * LANES].astype(F32)
        if (h % 2) != half:
            slab = pltpu.roll(slab, HEAD_DIM, 1)
        keep = low if half == 0 else jnp.logical_not(low)
        out.append(jnp.where(keep, slab, 0.0).astype(BF16))
    return jnp.concatenate(out, axis=0)


def _attn_kernel(q_ref, iq_ref, ikwf_ref, k_ref, va_ref, ikb_ref, tri_ref, o_ref,
                 sc_ref, m_ref, acc_ref, *, tq, n_keys, causal, ktop):
    i = pl.program_id(1)
    sub = KEY_TILE // LANES
    if causal:
        nt = lax.shift_right_logical(i * tq + tq + KEY_TILE - 1, 8)
    else:
        nt = (n_keys + KEY_TILE - 1) // KEY_TILE
    rows = lax.broadcasted_iota(jnp.int32, (tq, LANES), 0)
    lanes = lax.broadcasted_iota(jnp.int32, (tq, LANES), 1)

    iqs = _stack_heads(iq_ref, tq, [(h, 0) for h in range(N_IDX_HEADS)])
    w8 = ikwf_ref[0][:, IDX_DIM:IDX_DIM + N_IDX_HEADS]
    wb = [jnp.broadcast_to(w8[:, h:h + 1], (tq, LANES)) for h in range(N_IDX_HEADS)]

    def score_body(t, carry):
        kt = ikb_ref[0, pl.ds(pl.multiple_of(t * KEY_TILE, KEY_TILE), KEY_TILE), :]
        d = lax.dot_general(iqs, kt, NT_DIMS, preferred_element_type=F32)
        for c in range(sub):
            s = jnp.zeros((tq, LANES), F32)
            for h in range(N_IDX_HEADS):
                s = s + jnp.maximum(d[h * tq:(h + 1) * tq, c * LANES:(c + 1) * LANES], 0.0) * wb[h]
            kpos = t * KEY_TILE + c * LANES + lanes
            if causal:
                adm = lax.shift_right_logical(kpos, 6) <= lax.shift_right_logical(i * tq + rows, 6)
            else:
                adm = kpos < n_keys
            sc_ref[t * sub + c] = jnp.where(adm, s, -jnp.inf)
        return carry

    lax.fori_loop(0, nt, score_body, 0)

    def count(pred_fn):
        def body(t, acc):
            for c in range(sub):
                acc = acc + jnp.where(pred_fn(sc_ref[t * sub + c]), 1.0, 0.0)
            return acc
        acc = lax.fori_loop(0, nt, body, jnp.zeros((tq, LANES), F32))
        return jnp.sum(acc, axis=1, keepdims=True)

    def key_to_float(key):
        bits = jnp.where(key >= 0, key, key ^ jnp.int32(0x7FFFFFFF))
        return lax.bitcast_convert_type(bits, F32)

    def bit_body(b, key):
        cand = key + lax.shift_left(jnp.int32(1), 31 - b)
        thr_b = jnp.broadcast_to(key_to_float(cand), (tq, LANES))
        cnt = count(lambda s: s >= thr_b)
        return jnp.where(cnt >= ktop, cand, key)

    key = lax.fori_loop(0, 32, bit_body, jnp.full((tq, 1), INT_MIN, jnp.int32))
    thr = jnp.where(key == INT_MIN, -3.0e38, key_to_float(key))
    thr_b = jnp.broadcast_to(thr, (tq, LANES))
    need_b = jnp.broadcast_to(ktop - count(lambda s: s > thr_b), (tq, LANES))

    tri = tri_ref[...]

    def bias_body(j, seen):
        s = sc_ref[j]
        eq = s == thr_b
        pre = jnp.dot(jnp.where(eq, 1.0, 0.0).astype(BF16), tri, preferred_element_type=F32)
        sel = (s > thr_b) | (eq & (pre + seen <= need_b))
        sc_ref[j] = jnp.where(sel, 0.0, NEG_BIAS)
        return seen + jnp.broadcast_to(pre[:, LANES - 1:LANES], (tq, LANES))

    lax.fori_loop(0, nt * sub, bias_body, jnp.zeros((tq, LANES), F32))

    qs = [_stack_heads(q_ref, tq, [(g * KV_GROUP + hh, g) for hh in range(KV_GROUP)])
          for g in range(N_KV_HEADS)]
    m_ref[...] = jnp.full(m_ref.shape, NEG_BIAS, F32)
    acc_ref[...] = jnp.zeros(acc_ref.shape, F32)

    def logits(t, g):
        kt = k_ref[0, pl.ds(pl.multiple_of(t * KEY_TILE, KEY_TILE), KEY_TILE), :]
        return lax.dot_general(qs[g], kt, NT_DIMS, preferred_element_type=F32)

    def max_body(t, carry):
        for g in range(N_KV_HEADS):
            lg = logits(t, g)
            for hh in range(KV_GROUP):
                r0 = hh * tq
                mm = m_ref[g, r0:r0 + tq]
                for c in range(sub):
                    mm = jnp.maximum(mm, lg[r0:r0 + tq, c * LANES:(c + 1) * LANES] + sc_ref[t * sub + c])
                m_ref[g, r0:r0 + tq] = mm
        return carry

    lax.fori_loop(0, nt, max_body, 0)
    for g in range(N_KV_HEADS):
        mrow = jnp.max(m_ref[g], axis=1, keepdims=True)
        m_ref[g] = jnp.broadcast_to(mrow, m_ref.shape[1:])

    def pv_body(t, carry):
        for g in range(N_KV_HEADS):
            lg = logits(t, g)
            ps = []
            for hh in range(KV_GROUP):
                r0 = hh * tq
                mm = m_ref[g, r0:r0 + tq]
                ps.append(jnp.concatenate(
                    [jnp.exp(lg[r0:r0 + tq, c * LANES:(c + 1) * LANES] + sc_ref[t * sub + c] - mm)
                     for c in range(sub)], axis=1).astype(BF16))
            p = jnp.concatenate(ps, axis=0)
            va = va_ref[0, pl.ds(pl.multiple_of(t * KEY_TILE, KEY_TILE), KEY_TILE), g * LANES:(g + 1) * LANES]
            acc_ref[g] += jnp.dot(p, va, preferred_element_type=F32)
        return carry

    lax.fori_loop(0, nt, pv_body, 0)

    outs = []
    for h in range(N_HEADS):
        g, hh = divmod(h, KV_GROUP)
        a = acc_ref[g, hh * tq:(hh + 1) * tq]
        outs.append(a[:, :HEAD_DIM] / a[:, HEAD_DIM:HEAD_DIM + 1])
    o_ref[0] = jnp.concatenate(outs, axis=1).astype(o_ref.dtype)


def _attn_call(q, iq, ikwf, k_all, va_all, ikb_all, tri, *, tq, n_keys, causal):
    b, t, _ = q.shape
    s_pad = k_all.shape[1]
    ktop = min(TOPK_MAX, n_keys // 4)
    qspec = lambda n: pl.BlockSpec((1, tq, n), lambda bi, i: (bi, i, 0))
    kspec = lambda n: pl.BlockSpec((1, s_pad, n), lambda bi, i: (bi, 0, 0))
    return pl.pallas_call(
        functools.partial(_attn_kernel, tq=tq, n_keys=n_keys, causal=causal, ktop=ktop),
        grid=(b, t // tq),
        in_specs=[qspec(D_ATTN), qspec(D_ATTN), qspec(LANES), kspec(LANES), kspec(2 * LANES),
                  kspec(LANES), pl.BlockSpec((LANES, LANES), lambda bi, i: (0, 0))],
        out_specs=qspec(D_ATTN),
        out_shape=jax.ShapeDtypeStruct((b, t, D_ATTN), BF16),
        scratch_shapes=[
            pltpu.VMEM((s_pad // LANES, tq, LANES), F32),
            pltpu.VMEM((N_KV_HEADS, KV_GROUP * tq, LANES), F32),
            pltpu.VMEM((N_KV_HEADS, KV_GROUP * tq, LANES), F32),
        ],
        compiler_params=_cparams(("arbitrary", "arbitrary")),
        name="dsa_attention",
    )(q, iq, ikwf, k_all, va_all, ikb_all, tri)


CONV_HALO = 32
CONV_ROWS = 32


def _conv_kernel(g_ref, past_ref, w_ref, b_ref, lg_ref, lb_ref, y_ref, win_ref, *, tt):
    j = pl.program_id(1)

    @pl.when(j == 0)
    def _():
        win_ref[0:CONV_HALO] = past_ref[0]

    @pl.when(j > 0)
    def _():
        win_ref[0:CONV_HALO] = win_ref[tt:tt + CONV_HALO]

    win_ref[CONV_HALO:CONV_HALO + tt] = g_ref[0]
    off = CONV_HALO - (CONV_W - 1)
    bias = b_ref[...]
    lg = lg_ref[...]
    lb = lb_ref[...]
    for r0 in range(0, tt, CONV_ROWS):
        acc = jnp.zeros((CONV_ROWS, D_CONV), F32)
        for tap in range(CONV_W):
            acc = acc + win_ref[r0 + off + tap:r0 + off + tap + CONV_ROWS] * w_ref[tap:tap + 1]
        acc = acc + bias
        mu = jnp.mean(acc, axis=-1, keepdims=True)
        xc = acc - mu
        var = jnp.mean(xc * xc, axis=-1, keepdims=True)
        z = xc * lax.rsqrt(var + EPS) * lg + lb
        y_ref[0, r0:r0 + CONV_ROWS] = (z * jax.nn.sigmoid(z)).astype(y_ref.dtype)


def _conv_call(g, past32, conv_w, conv_b, ln_g, ln_b, *, tt):
    nb, r, _ = g.shape
    vec = pl.BlockSpec((1, D_CONV), lambda bi, j: (0, 0))
    return pl.pallas_call(
        functools.partial(_conv_kernel, tt=tt),
        grid=(nb, r // tt),
        in_specs=[
            pl.BlockSpec((1, tt, D_CONV), lambda bi, j: (bi, j, 0)),
            pl.BlockSpec((1, CONV_HALO, D_CONV), lambda bi, j: (bi, 0, 0)),
            pl.BlockSpec((CONV_W, D_CONV), lambda bi, j: (0, 0)),
            vec, vec, vec,
        ],
        out_specs=pl.BlockSpec((1, tt, D_CONV), lambda bi, j: (bi, j, 0)),
        out_shape=jax.ShapeDtypeStruct((nb, r, D_CONV), BF16),
        scratch_shapes=[pltpu.VMEM((tt + CONV_HALO, D_CONV), F32)],
        compiler_params=_cparams(("arbitrary", "arbitrary")),
        name="conv_module",
    )(g, past32, conv_w, conv_b, ln_g, ln_b)


def _first_argmax(v, width):
    idx = lax.broadcasted_iota(jnp.int32, v.shape, 1)
    mx = jnp.max(v, axis=1, keepdims=True)
    first = jnp.min(jnp.where(v == mx, idx, width), axis=1, keepdims=True)
    return mx, first


def _post_kernel(x_ref, at_ref, y_ref, gta_ref, shf_ref, scf_ref, wo_ref, gn_ref, wr_ref, br_ref,
                 x1_ref, h2_ref, gate_ref, *, bb, rt):
    m = bb * rt
    at = at_ref[...].reshape(m, D_ATTN)
    yy = y_ref[...].reshape(m, D_CONV)
    mixo = (jnp.dot(at, wo_ref[0:D_ATTN], preferred_element_type=F32)
            + jnp.dot(yy, wo_ref[D_ATTN:D_ATTN + D_CONV], preferred_element_type=F32))
    x1 = x_ref[...] + gta_ref[...] * mixo.reshape(bb, rt, D_MODEL)
    x1_ref[...] = x1
    ms = jnp.mean(x1 * x1, axis=-1, keepdims=True)
    h = x1 * lax.rsqrt(ms + EPS) * gn_ref[...]
    h = h * (1.0 + scf_ref[...]) + shf_ref[...]
    hb = h.reshape(m, D_MODEL).astype(BF16)
    h2_ref[...] = hb.reshape(bb, rt, D_MODEL)

    logit = jnp.dot(hb, wr_ref[...], preferred_element_type=F32) + br_ref[...]
    gl = logit[:, 0:N_GROUPS]
    el = logit[:, N_GROUPS:N_GROUPS + N_EXPERTS]
    gmax, gsel = _first_argmax(gl, N_GROUPS)
    p_g = 1.0 / jnp.sum(jnp.exp(gl - gmax), axis=1, keepdims=True)
    ein = jnp.zeros((m, EXPERTS_PER_GROUP), F32)
    for gi in range(N_GROUPS):
        ein = ein + jnp.where(gsel == gi, el[:, gi * EXPERTS_PER_GROUP:(gi + 1) * EXPERTS_PER_GROUP], 0.0)
    e1, i1 = _first_argmax(ein, EXPERTS_PER_GROUP)
    idx4 = lax.broadcasted_iota(jnp.int32, ein.shape, 1)
    e2, i2 = _first_argmax(jnp.where(idx4 == i1, -jnp.inf, ein), EXPERTS_PER_GROUP)
    r2 = jnp.exp(e2 - e1)
    w1 = p_g / (1.0 + r2)
    w2 = p_g * r2 / (1.0 + r2)
    lane = lax.broadcasted_iota(jnp.int32, (m, LANES), 1)
    base = gsel * EXPERTS_PER_GROUP
    gate = jnp.where(lane == base + i1, w1, 0.0) + jnp.where(lane == base + i2, w2, 0.0)
    gate_ref[...] = gate.reshape(bb, rt, LANES)


def _post_call(x, attn, y, mod, w_out, gn, w_r, b_r, *, bb, rt):
    nb, r, d = x.shape
    tspec = lambda n: pl.BlockSpec((bb, rt, n), lambda i, j: (i, j, 0))
    mspec = lambda c: pl.BlockSpec((bb, 1, d), lambda i, j: (i, 0, c))
    full = lambda a, b_: pl.BlockSpec((a, b_), lambda i, j: (0, 0))
    return pl.pallas_call(
        functools.partial(_post_kernel, bb=bb, rt=rt),
        grid=(nb // bb, r // rt),
        in_specs=[tspec(d), tspec(D_ATTN), tspec(D_CONV), mspec(2), mspec(3), mspec(4),
                  full(D_ATTN + D_CONV, d), full(1, d), full(d, LANES), full(1, LANES)],
        out_specs=[tspec(d), tspec(d), tspec(LANES)],
        out_shape=[jax.ShapeDtypeStruct((nb, r, d), F32), jax.ShapeDtypeStruct((nb, r, d), BF16),
                   jax.ShapeDtypeStruct((nb, r, LANES), F32)],
        compiler_params=_cparams(("arbitrary", "arbitrary")),
        name="out_proj_router",
    )(x, attn, y, mod, mod, mod, w_out, gn, w_r, b_r)


def _moe_kernel(h_ref, gate_ref, x1_ref, gtf_ref, wg_ref, wu_ref, wd_ref, o_ref, acc_ref, *, bb, rt):
    e = pl.program_id(2)
    m = bb * rt

    @pl.when(e == 0)
    def _():
        acc_ref[...] = jnp.zeros(acc_ref.shape, F32)

    hb = h_ref[...].reshape(m, D_MODEL)
    gate = gate_ref[...].reshape(m, LANES)
    lane = lax.broadcasted_iota(jnp.int32, (m, LANES), 1)
    ge = jnp.sum(jnp.where(lane == e, gate, 0.0), axis=1, keepdims=True)
    a = jnp.dot(hb, wg_ref[0], preferred_element_type=F32)
    u = jnp.dot(hb, wu_ref[0], preferred_element_type=F32)
    hid = (a * jax.nn.sigmoid(a)) * u * ge
    acc_ref[...] += jnp.dot(hid.astype(BF16), wd_ref[0], preferred_element_type=F32)

    @pl.when(e == N_EXPERTS - 1)
    def _():
        o_ref[...] = x1_ref[...] + gtf_ref[...] * acc_ref[...].reshape(bb, rt, D_MODEL)


def _moe_call(h2, gate, x1, mod, wg, wu, wd, *, bb, rt):
    nb, r, d = x1.shape
    tspec = lambda n: pl.BlockSpec((bb, rt, n), lambda i, j, e: (i, j, 0))
    return pl.pallas_call(
        functools.partial(_moe_kernel, bb=bb, rt=rt),
        grid=(nb // bb, r // rt, N_EXPERTS),
        in_specs=[tspec(d), tspec(LANES), tspec(d),
                  pl.BlockSpec((bb, 1, d), lambda i, j, e: (i, 0, 5)),
                  pl.BlockSpec((1, d, D_EXPERT), lambda i, j, e: (e, 0, 0)),
                  pl.BlockSpec((1, d, D_EXPERT), lambda i, j, e: (e, 0, 0)),
                  pl.BlockSpec((1, D_EXPERT, d), lambda i, j, e: (e, 0, 0))],
        out_specs=tspec(d),
        out_shape=jax.ShapeDtypeStruct((nb, r, d), F32),
        scratch_shapes=[pltpu.VMEM((bb * rt, d), F32)],
        compiler_params=_cparams(("arbitrary", "arbitrary", "arbitrary")),
        name="moe",
    )(h2, gate, x1, mod, wg, wu, wd)


def _rope_tables(pos):
    half = ROPE_DIM // 2
    inv = ROPE_THETA ** (-jnp.arange(half, dtype=F32) / half)
    ang = pos.astype(F32)[:, None] * inv[None, :]
    cos, sin = jnp.cos(ang), jnp.sin(ang)
    r = pos.shape[0]
    z = lambda n: jnp.zeros((r, n), F32)
    c64 = jnp.concatenate([cos, cos, jnp.ones((r, HEAD_DIM - ROPE_DIM), F32)], axis=1)
    sa64 = jnp.concatenate([-sin, z(HEAD_DIM - half)], axis=1)
    sb64 = jnp.concatenate([z(half), sin, z(HEAD_DIM - ROPE_DIM)], axis=1)
    iw_scale = jnp.full((r, N_IDX_HEADS), N_IDX_HEADS ** -0.5 * IDX_DIM ** -0.5, F32)
    ci = jnp.concatenate([c64, iw_scale, z(LANES - IDX_DIM - N_IDX_HEADS)], axis=1)
    two = lambda a: jnp.concatenate([a, a], axis=1)
    pad = lambda a: jnp.concatenate([a, z(LANES - HEAD_DIM)], axis=1)
    return jnp.stack([two(c64), two(sa64), two(sb64), ci, pad(sa64), pad(sb64)])


def _layer_weights(l, w_in, q_norm_g, k_norm_g, conv_w, w_out, w_router_group, b_router_group,
                   w_router_expert, b_router_expert, w_gate, w_up, w_down):
    d = D_MODEL
    w_cat = jnp.concatenate(
        [w_in[l][:, :IN_SPLIT], jnp.zeros((d, COL_A - IN_SPLIT), F32), w_in[l][:, IN_SPLIT:]], axis=1).astype(BF16)
    gqk = jnp.stack([jnp.tile(q_norm_g[l], 2), jnp.tile(k_norm_g[l], 2)])
    w_r = jnp.concatenate([w_router_group[l], w_router_expert[l],
                           jnp.zeros((d, LANES - N_GROUPS - N_EXPERTS), F32)], axis=1).astype(BF16)
    b_r = jnp.concatenate([b_router_group[l], b_router_expert[l],
                           jnp.zeros((LANES - N_GROUPS - N_EXPERTS,), F32)]).reshape(1, LANES)
    return dict(w_cat=w_cat, gqk=gqk, w_out=w_out[l].astype(BF16), w_r=w_r, b_r=b_r,
                wg=w_gate[l].astype(BF16), wu=w_up[l].astype(BF16), wd=w_down[l].astype(BF16))


def _trunk_layer(x, mod, tabs, past, lw, gn_mix, gn_ffn, conv_w, conv_b, ln_g, ln_b, consts,
                 *, bb, rt, tq, tt, mbb, mrt):
    nb, r, d = x.shape
    bd, tri = consts
    q, iq, kf, vf, kb, va, ikwf, ikwb, g = _inproj_call(
        x, mod, gn_mix.reshape(1, d), lw["w_cat"], lw["gqk"], tabs, bd, bb=bb, rt=rt, tab_blocked=past is None)
    if past is None:
        k_all, va_all, ikb_all, n_keys = kb, va, ikwb, r
        past32 = jnp.zeros((nb, CONV_HALO, D_CONV), F32)
        padded_tail = g[:, r - (CONV_W - 1):]
    else:
        k_past, v_past, ik_past, conv_past = past
        past_len = k_past.shape[1]
        n_keys = past_len + r
        s_pad = -(-n_keys // KEY_TILE) * KEY_TILE
        tail = s_pad - n_keys
        ones = jnp.ones((nb, past_len, N_KV_HEADS, HEAD_DIM), BF16)
        va_past = jnp.concatenate([v_past.astype(BF16), ones], axis=-1).reshape(nb, past_len, 2 * LANES)
        ikw_past = jnp.concatenate([ik_past.astype(BF16), jnp.zeros((nb, past_len, LANES - IDX_DIM), BF16)], axis=-1)
        zpad = lambda n: jnp.zeros((nb, tail, n), BF16)
        k_all = jnp.concatenate([k_past.reshape(nb, past_len, LANES).astype(BF16), kb, zpad(LANES)], axis=1)
        va_all = jnp.concatenate([va_past, va, zpad(2 * LANES)], axis=1)
        ikb_all = jnp.concatenate([ikw_past, ikwb, zpad(LANES)], axis=1)
        past32 = jnp.concatenate([jnp.zeros((nb, CONV_HALO - (CONV_W - 1), D_CONV), F32), conv_past], axis=1)
        padded_tail = jnp.concatenate([conv_past, g], axis=1)[:, -(CONV_W - 1):]
    attn = _attn_call(q, iq, ikwf, k_all, va_all, ikb_all, tri, tq=tq, n_keys=n_keys, causal=past is None)
    y = _conv_call(g, past32, conv_w, conv_b.reshape(1, -1), ln_g.reshape(1, -1), ln_b.reshape(1, -1), tt=tt)
    x1, h2, gate = _post_call(x, attn, y, mod, lw["w_out"], gn_ffn.reshape(1, d), lw["w_r"], lw["b_r"], bb=bb, rt=rt)
    x2 = _moe_call(h2, gate, x1, mod, lw["wg"], lw["wu"], lw["wd"], bb=mbb, rt=mrt)
    new_k = kf.reshape(nb, r, N_KV_HEADS, HEAD_DIM)
    new_v = vf.reshape(nb, r, N_KV_HEADS, HEAD_DIM)
    return x2, (new_k, new_v, ikwf[:, :, :IDX_DIM], padded_tail)


def kernel(x_prompt, x_sample, c_prompt, c_sample, cache_k, cache_v, cache_idx_k, state_conv, w_ada, b_ada, g_norm_mix, w_in, q_norm_g, k_norm_g, conv_w, conv_b, ln_conv_g, ln_conv_b, w_out, g_norm_ffn, w_router_group, b_router_group, w_router_expert, b_router_expert, w_gate, w_up, w_down):
    depth = w_ada.shape[0]
    bp, seq, d = x_prompt.shape
    bs, dseq, _ = x_sample.shape
    past_len = cache_k.shape[2]

    n_c = bp + bs
    n_cp = -(-n_c // SUBLANES) * SUBLANES
    c_all = jnp.concatenate([c_prompt, c_sample, jnp.zeros((n_cp - n_c, d), F32)], axis=0)
    mod_all = _ada_call(c_all, w_ada, b_ada)

    blk = np.arange(LANES) // HEAD_DIM
    bd = jnp.asarray(blk[:, None] == blk[None, :], BF16)
    tri = jnp.asarray(np.arange(LANES)[:, None] <= np.arange(LANES)[None, :], BF16)
    consts = (bd, tri)

    tabs_p = _rope_tables(jnp.arange(seq, dtype=jnp.int32))
    tabs_s = _rope_tables(past_len + jnp.arange(dseq, dtype=jnp.int32))

    xp, xs = x_prompt, x_sample
    st_p, st_s = [], []
    for l in range(depth):
        lw = _layer_weights(l, w_in, q_norm_g, k_norm_g, conv_w, w_out, w_router_group, b_router_group,
                            w_router_expert, b_router_expert, w_gate, w_up, w_down)
        common = (lw, g_norm_mix[l], g_norm_ffn[l], conv_w[l], conv_b[l], ln_conv_g[l], ln_conv_b[l], consts)
        mod_p = mod_all[l, :bp].reshape(bp, 1, 6 * d)
        mod_s = mod_all[l, bp:bp + bs].reshape(bs, 1, 6 * d)
        xp, st = _trunk_layer(xp, mod_p, tabs_p, None, *common,
                              bb=1, rt=512, tq=128, tt=256, mbb=1, mrt=1024)
        st_p.append(st)
        past = (cache_k[l], cache_v[l], cache_idx_k[l], state_conv[l])
        xs, st = _trunk_layer(xs, mod_s, tabs_s, past, *common,
                              bb=8, rt=dseq, tq=dseq, tt=dseq, mbb=16, mrt=dseq)
        st_s.append(st)

    stack = lambda sts, i: jnp.stack([s[i] for s in sts])
    return (xp, xs, stack(st_p, 0), stack(st_p, 1), stack(st_p, 2), stack(st_p, 3),
            stack(st_s, 0), stack(st_s, 1), stack(st_s, 2), stack(st_s, 3))
```

```python
import functools

import jax
import jax.numpy as jnp
import numpy as np
from jax import lax
from jax.experimental import pallas as pl
from jax.experimental.pallas import tpu as pltpu

D_MODEL = 1024
CHUNK = 64
N_HEADS = 8
HEAD_DIM = 64
N_KV_HEADS = 2
KV_GROUP = N_HEADS // N_KV_HEADS
ROPE_DIM = HEAD_DIM // 4
ROPE_THETA = 500000.0
N_IDX_HEADS = 8
IDX_DIM = 64
TOPK_MAX = 256
D_ATTN = N_HEADS * HEAD_DIM
D_CONV = 512
CONV_W = 31
N_GROUPS = 4
EXPERTS_PER_GROUP = 4
N_EXPERTS = N_GROUPS * EXPERTS_PER_GROUP
D_EXPERT = 256
EPS = 1e-6

LANES = 128
SUBLANES = 8
KEY_TILE = 256
NEG_BIAS = -1e30
SEL_MARK = 3.0e38
INT_MIN = -(2 ** 31)
VMEM_LIMIT = 48 * 1024 * 1024

COL_Q = 0
COL_K = 512
COL_V = 640
COL_IQ = 768
COL_IKW = 1280
COL_A = 1408
COL_B = 1920
W_CAT_COLS = 2432
IN_SPLIT = 1352

F32 = jnp.float32
BF16 = jnp.bfloat16
NT_DIMS = (((1,), (1,)), ((), ()))


def _cparams(sem):
    return pltpu.CompilerParams(dimension_semantics=sem, vmem_limit_bytes=VMEM_LIMIT)


def _ada_kernel(c_ref, w_ref, b_ref, o_ref):
    c = c_ref[...]
    s = (c * jax.nn.sigmoid(c)).astype(BF16)
    o_ref[0] = jnp.dot(s, w_ref[0].astype(BF16), preferred_element_type=F32) + b_ref[0]


def _ada_call(c_all, w_ada, b_ada):
    depth, d, n6 = w_ada.shape
    bp = c_all.shape[0]
    tn = 1536
    return pl.pallas_call(
        _ada_kernel,
        grid=(depth, n6 // tn),
        in_specs=[
            pl.BlockSpec((bp, d), lambda l, j: (0, 0)),
            pl.BlockSpec((1, d, tn), lambda l, j: (l, 0, j)),
            pl.BlockSpec((1, 1, tn), lambda l, j: (l, 0, j)),
        ],
        out_specs=pl.BlockSpec((1, bp, tn), lambda l, j: (l, 0, j)),
        out_shape=jax.ShapeDtypeStruct((depth, bp, n6), F32),
        compiler_params=_cparams(("arbitrary", "arbitrary")),
        name="ada_mod",
    )(c_all, w_ada, b_ada.reshape(depth, 1, n6))


def _head_mean_sq(y, bd):
    sq = y * y
    hi = sq.astype(BF16)
    lo = (sq - hi.astype(F32)).astype(BF16)
    s = jnp.dot(hi, bd, preferred_element_type=F32) + jnp.dot(lo, bd, preferred_element_type=F32)
    return s * (1.0 / HEAD_DIM)


def _rope(y, c, sa, sb):
    up = pltpu.roll(y, LANES - ROPE_DIM // 2, 1)
    dn = pltpu.roll(y, ROPE_DIM // 2, 1)
    return y * c + up * sa + dn * sb


def _inproj_kernel(x_ref, sh_ref, sc_ref, gn_ref, w_ref, gqk_ref, tabs_ref, bd_ref,
                   q_ref, iq_ref, kf_ref, vf_ref, kb_ref, va_ref, ikwf_ref, ikwb_ref, g_ref,
                   *, bb, rt):
    m = bb * rt
    x = x_ref[...]
    ms = jnp.mean(x * x, axis=-1, keepdims=True)
    h = x * lax.rsqrt(ms + EPS) * gn_ref[...]
    h = h * (1.0 + sc_ref[...]) + sh_ref[...]
    hb = h.reshape(m, D_MODEL).astype(BF16)

    def tab(i):
        t = tabs_ref[i]
        if bb > 1:
            t = jnp.broadcast_to(t[None], (bb, rt, LANES)).reshape(m, LANES)
        return t

    cq, saq, sbq, ci, sai, sbi = (tab(i) for i in range(6))
    bd = bd_ref[...]
    gq = gqk_ref[0:1, :]
    gk = gqk_ref[1:2, :]

    def proj(c0, n):
        return jnp.dot(hb, w_ref[:, c0:c0 + n], preferred_element_type=F32)

    def put(ref, c0, val):
        ref[:, :, c0:c0 + val.shape[-1]] = val.reshape(bb, rt, val.shape[-1]).astype(ref.dtype)

    pq = proj(COL_Q, D_ATTN)
    for j in range(D_ATTN // LANES):
        y = pq[:, j * LANES:(j + 1) * LANES]
        y = y * lax.rsqrt(_head_mean_sq(y, bd) + EPS) * gq
        put(q_ref, j * LANES, _rope(y, cq, saq, sbq) * (HEAD_DIM ** -0.5))

    pkv = proj(COL_K, 2 * LANES)
    y = pkv[:, :LANES]
    y = y * lax.rsqrt(_head_mean_sq(y, bd) + EPS) * gk
    kk = _rope(y, cq, saq, sbq)
    put(kf_ref, 0, kk)
    put(kb_ref, 0, kk)
    vv = pkv[:, LANES:]
    put(vf_ref, 0, vv)
    lane = lax.broadcasted_iota(jnp.int32, (m, LANES), 1)
    put(va_ref, 0, jnp.where(lane < HEAD_DIM, vv, 1.0))
    put(va_ref, LANES, jnp.where(lane < HEAD_DIM, pltpu.roll(vv, HEAD_DIM, 1), 1.0))

    piq = proj(COL_IQ, N_IDX_HEADS * IDX_DIM)
    for j in range(N_IDX_HEADS * IDX_DIM // LANES):
        put(iq_ref, j * LANES, _rope(piq[:, j * LANES:(j + 1) * LANES], cq, saq, sbq))

    ikw = _rope(proj(COL_IKW, LANES), ci, sai, sbi)
    put(ikwf_ref, 0, ikw)
    put(ikwb_ref, 0, ikw)

    pu = proj(COL_A, 2 * D_CONV)
    put(g_ref, 0, pu[:, :D_CONV] * jax.nn.sigmoid(pu[:, D_CONV:]))


def _inproj_call(x, mod, gn, w_cat, gqk, tabs, bd, *, bb, rt, tab_blocked):
    nb, r, d = x.shape
    grid = (nb // bb, r // rt)
    tok = lambda n, dt: jax.ShapeDtypeStruct((nb, r, n), dt)
    tspec = lambda n: pl.BlockSpec((bb, rt, n), lambda i, j: (i, j, 0))
    return pl.pallas_call(
        functools.partial(_inproj_kernel, bb=bb, rt=rt),
        grid=grid,
        in_specs=[
            tspec(d),
            pl.BlockSpec((bb, 1, d), lambda i, j: (i, 0, 0)),
            pl.BlockSpec((bb, 1, d), lambda i, j: (i, 0, 1)),
            pl.BlockSpec((1, d), lambda i, j: (0, 0)),
            pl.BlockSpec((d, W_CAT_COLS), lambda i, j: (0, 0)),
            pl.BlockSpec((2, LANES), lambda i, j: (0, 0)),
            pl.BlockSpec((6, rt, LANES), (lambda i, j: (0, j, 0)) if tab_blocked else (lambda i, j: (0, 0, 0))),
            pl.BlockSpec((LANES, LANES), lambda i, j: (0, 0)),
        ],
        out_specs=[tspec(D_ATTN), tspec(D_ATTN), tspec(LANES), tspec(LANES), tspec(LANES),
                   tspec(2 * LANES), tspec(LANES), tspec(LANES), tspec(D_CONV)],
        out_shape=[tok(D_ATTN, BF16), tok(D_ATTN, BF16), tok(LANES, F32), tok(LANES, F32),
                   tok(LANES, BF16), tok(2 * LANES, BF16), tok(LANES, F32), tok(LANES, BF16),
                   tok(D_CONV, F32)],
        compiler_params=_cparams(("arbitrary", "arbitrary")),
        name="in_proj",
    )(x, mod, mod, gn, w_cat, gqk, tabs, bd)


def _stack_heads(ref, tq, heads):
    lane = lax.broadcasted_iota(jnp.int32, (tq, LANES), 1)
    low = lane < HEAD_DIM
    out = []
    for h, half in heads:
        slab = ref[0, :, (h // 2) * LANES:(h // 2 + 1) * LANES].astype(F32)
        if (h % 2) != half:
            slab = pltpu.roll(slab, HEAD_DIM, 1)
        keep = low if half == 0 else jnp.logical_not(low)
        out.append(jnp.where(keep, slab, 0.0).astype(BF16))
    return jnp.concatenate(out, axis=0)


def _tile_loop(nt, body, init):
    def pair(p, carry):
        return body(2 * p + 1, body(2 * p, carry))

    if isinstance(nt, int):
        carry = lax.fori_loop(0, nt // 2, pair, init)
        return body(nt - 1, carry) if nt % 2 else carry
    carry = lax.fori_loop(0, lax.shift_right_logical(nt, 1), pair, init)
    return lax.cond((nt & 1) == 1, lambda c: body(nt - 1, c), lambda c: c, carry)


def _attn_kernel(q_ref, iq_ref, ikwf_ref, k_ref, va_ref, ikb_ref, tri_ref, o_ref,
                 sc_ref, m_ref, acc_ref, *, tq, n_keys, causal, ktop):
    i = pl.program_id(1)
    sub = KEY_TILE // LANES
    if causal:
        nt = lax.shift_right_logical(i * tq + tq + KEY_TILE - 1, 8)
    else:
        nt = (n_keys + KEY_TILE - 1) // KEY_TILE
    rows = lax.broadcasted_iota(jnp.int32, (tq, LANES), 0)
    lanes = lax.broadcasted_iota(jnp.int32, (tq, LANES), 1)

    iqs = _stack_heads(iq_ref, tq, [(h, 0) for h in range(N_IDX_HEADS)])
    w8 = ikwf_ref[0][:, IDX_DIM:IDX_DIM + N_IDX_HEADS]
    wb = [jnp.broadcast_to(w8[:, h:h + 1], (tq, LANES)) for h in range(N_IDX_HEADS)]

    def score_body(t, carry):
        kt = ikb_ref[0, pl.ds(pl.multiple_of(t * KEY_TILE, KEY_TILE), KEY_TILE), :]
        d = lax.dot_general(iqs, kt, NT_DIMS, preferred_element_type=F32)
        for c in range(sub):
            s = jnp.zeros((tq, LANES), F32)
            for h in range(N_IDX_HEADS):
                s = s + jnp.maximum(d[h * tq:(h + 1) * tq, c * LANES:(c + 1) * LANES], 0.0) * wb[h]
            kpos = t * KEY_TILE + c * LANES + lanes
            if causal:
                adm = lax.shift_right_logical(kpos, 6) <= lax.shift_right_logical(i * tq + rows, 6)
            else:
                adm = kpos < n_keys
            sc_ref[t * sub + c] = jnp.where(adm, s, -jnp.inf)
        return carry

    _tile_loop(nt, score_body, 0)

    def count(pred_fn):
        def body(t, acc):
            for c in range(sub):
                acc = acc + jnp.where(pred_fn(sc_ref[t * sub + c]), 1.0, 0.0)
            return acc
        acc = _tile_loop(nt, body, jnp.zeros((tq, LANES), F32))
        return jnp.sum(acc, axis=1, keepdims=True)

    def key_to_float(key):
        bits = jnp.where(key >= 0, key, key ^ jnp.int32(0x7FFFFFFF))
        return lax.bitcast_convert_type(bits, F32)

    def bit_body(b, key):
        cand = key + lax.shift_left(jnp.int32(1), 31 - b)
        thr_b = jnp.broadcast_to(key_to_float(cand), (tq, LANES))
        cnt = count(lambda s: s >= thr_b)
        return jnp.where(cnt >= ktop, cand, key)

    key = lax.fori_loop(0, 32, bit_body, jnp.full((tq, 1), INT_MIN, jnp.int32))
    thr = jnp.where(key == INT_MIN, -3.0e38, key_to_float(key))
    thr_b = jnp.broadcast_to(thr, (tq, LANES))
    n_gt = count(lambda s: s > thr_b)
    n_ge = count(lambda s: s >= thr_b)
    has_ties = jnp.max(n_ge) > ktop

    @pl.when(has_ties)
    def _():
        tri = tri_ref[...]
        need_b = jnp.broadcast_to(ktop - n_gt, (tq, LANES))

        def tie_body(j, seen):
            s = sc_ref[j]
            eq = s == thr_b
            pre = jnp.dot(jnp.where(eq, 1.0, 0.0).astype(BF16), tri, preferred_element_type=F32)
            sel = (s > thr_b) | (eq & (pre + seen <= need_b))
            sc_ref[j] = jnp.where(sel, SEL_MARK, -SEL_MARK)
            return seen + jnp.broadcast_to(pre[:, LANES - 1:LANES], (tq, LANES))

        lax.fori_loop(0, nt * sub, tie_body, jnp.zeros((tq, LANES), F32))

    sel_thr = jnp.where(has_ties, 0.0, thr_b)

    def bias(t, c):
        return jnp.where(sc_ref[t * sub + c] >= sel_thr, 0.0, NEG_BIAS)

    qs = [_stack_heads(q_ref, tq, [(g * KV_GROUP + hh, g) for hh in range(KV_GROUP)])
          for g in range(N_KV_HEADS)]
    m_ref[...] = jnp.full(m_ref.shape, NEG_BIAS, F32)
    acc_ref[...] = jnp.zeros(acc_ref.shape, F32)

    def logits(t, g):
        kt = k_ref[0, pl.ds(pl.multiple_of(t * KEY_TILE, KEY_TILE), KEY_TILE), :]
        return lax.dot_general(qs[g], kt, NT_DIMS, preferred_element_type=F32)

    def max_body(t, carry):
        bs = [bias(t, c) for c in range(sub)]
        for g in range(N_KV_HEADS):
            lg = logits(t, g)
            for hh in range(KV_GROUP):
                r0 = hh * tq
                mm = m_ref[g, r0:r0 + tq]
                for c in range(sub):
                    mm = jnp.maximum(mm, lg[r0:r0 + tq, c * LANES:(c + 1) * LANES] + bs[c])
                m_ref[g, r0:r0 + tq] = mm
        return carry

    _tile_loop(nt, max_body, 0)
    for g in range(N_KV_HEADS):
        mrow = jnp.max(m_ref[g], axis=1, keepdims=True)
        m_ref[g] = jnp.broadcast_to(mrow, m_ref.shape[1:])

    def pv_body(t, carry):
        bs = [bias(t, c) for c in range(sub)]
        for g in range(N_KV_HEADS):
            lg = logits(t, g)
            ps = []
            for hh in range(KV_GROUP):
                r0 = hh * tq
                mm = m_ref[g, r0:r0 + tq]
                ps.append(jnp.concatenate(
                    [jnp.exp(lg[r0:r0 + tq, c * LANES:(c + 1) * LANES] + bs[c] - mm)
                     for c in range(sub)], axis=1).astype(BF16))
            p = jnp.concatenate(ps, axis=0)
            va = va_ref[0, pl.ds(pl.multiple_of(t * KEY_TILE, KEY_TILE), KEY_TILE), g * LANES:(g + 1) * LANES]
            acc_ref[g] += jnp.dot(p, va, preferred_element_type=F32)
        return carry

    _tile_loop(nt, pv_body, 0)

    outs = []
    for h in range(N_HEADS):
        g, hh = divmod(h, KV_GROUP)
        a = acc_ref[g, hh * tq:(hh + 1) * tq]
        outs.append(a[:, :HEAD_DIM] / a[:, HEAD_DIM:HEAD_DIM + 1])
    o_ref[0] = jnp.concatenate(outs, axis=1).astype(o_ref.dtype)


def _attn_call(q, iq, ikwf, k_all, va_all, ikb_all, tri, *, tq, n_keys, causal):
    b, t, _ = q.shape
    s_pad = k_all.shape[1]
    ktop = min(TOPK_MAX, n_keys // 4)
    qspec = lambda n: pl.BlockSpec((1, tq, n), lambda bi, i: (bi, i, 0))
    kspec = lambda n: pl.BlockSpec((1, s_pad, n), lambda bi, i: (bi, 0, 0))
    return pl.pallas_call(
        functools.partial(_attn_kernel, tq=tq, n_keys=n_keys, causal=causal, ktop=ktop),
        grid=(b, t // tq),
        in_specs=[qspec(D_ATTN), qspec(D_ATTN), qspec(LANES), kspec(LANES), kspec(2 * LANES),
                  kspec(LANES), pl.BlockSpec((LANES, LANES), lambda bi, i: (0, 0))],
        out_specs=qspec(D_ATTN),
        out_shape=jax.ShapeDtypeStruct((b, t, D_ATTN), BF16),
        scratch_shapes=[
            pltpu.VMEM((s_pad // LANES, tq, LANES), F32),
            pltpu.VMEM((N_KV_HEADS, KV_GROUP * tq, LANES), F32),
            pltpu.VMEM((N_KV_HEADS, KV_GROUP * tq, LANES), F32),
        ],
        compiler_params=_cparams(("arbitrary", "arbitrary")),
        name="dsa_attention",
    )(q, iq, ikwf, k_all, va_all, ikb_all, tri)


CONV_HALO = 32
CONV_ROWS = 32


def _conv_kernel(g_ref, past_ref, w_ref, b_ref, lg_ref, lb_ref, y_ref, win_ref, *, tt):
    j = pl.program_id(1)

    @pl.when(j == 0)
    def _():
        win_ref[0:CONV_HALO] = past_ref[0]

    @pl.when(j > 0)
    def _():
        win_ref[0:CONV_HALO] = win_ref[tt:tt + CONV_HALO]

    win_ref[CONV_HALO:CONV_HALO + tt] = g_ref[0]
    off = CONV_HALO - (CONV_W - 1)
    bias = b_ref[...]
    lg = lg_ref[...]
    lb = lb_ref[...]
    for r0 in range(0, tt, CONV_ROWS):
        acc = jnp.zeros((CONV_ROWS, D_CONV), F32)
        for tap in range(CONV_W):
            acc = acc + win_ref[r0 + off + tap:r0 + off + tap + CONV_ROWS] * w_ref[tap:tap + 1]
        acc = acc + bias
        mu = jnp.mean(acc, axis=-1, keepdims=True)
        xc = acc - mu
        var = jnp.mean(xc * xc, axis=-1, keepdims=True)
        z = xc * lax.rsqrt(var + EPS) * lg + lb
        y_ref[0, r0:r0 + CONV_ROWS] = (z * jax.nn.sigmoid(z)).astype(y_ref.dtype)


def _conv_call(g, past32, conv_w, conv_b, ln_g, ln_b, *, tt):
    nb, r, _ = g.shape
    vec = pl.BlockSpec((1, D_CONV), lambda bi, j: (0, 0))
    return pl.pallas_call(
        functools.partial(_conv_kernel, tt=tt),
        grid=(nb, r // tt),
        in_specs=[
            pl.BlockSpec((1, tt, D_CONV), lambda bi, j: (bi, j, 0)),
            pl.BlockSpec((1, CONV_HALO, D_CONV), lambda bi, j: (bi, 0, 0)),
            pl.BlockSpec((CONV_W, D_CONV), lambda bi, j: (0, 0)),
            vec, vec, vec,
        ],
        out_specs=pl.BlockSpec((1, tt, D_CONV), lambda bi, j: (bi, j, 0)),
        out_shape=jax.ShapeDtypeStruct((nb, r, D_CONV), BF16),
        scratch_shapes=[pltpu.VMEM((tt + CONV_HALO, D_CONV), F32)],
        compiler_params=_cparams(("arbitrary", "arbitrary")),
        name="conv_module",
    )(g, past32, conv_w, conv_b, ln_g, ln_b)


def _first_argmax(v, width):
    idx = lax.broadcasted_iota(jnp.int32, v.shape, 1)
    mx = jnp.max(v, axis=1, keepdims=True)
    first = jnp.min(jnp.where(v == mx, idx, width), axis=1, keepdims=True)
    return mx, first


def _post_kernel(x_ref, at_ref, y_ref, gta_ref, shf_ref, scf_ref, wo_ref, gn_ref, wr_ref, br_ref,
                 x1_ref, h2_ref, gate_ref, *, bb, rt):
    m = bb * rt
    at = at_ref[...].reshape(m, D_ATTN)
    yy = y_ref[...].reshape(m, D_CONV)
    mixo = (jnp.dot(at, wo_ref[0:D_ATTN], preferred_element_type=F32)
            + jnp.dot(yy, wo_ref[D_ATTN:D_ATTN + D_CONV], preferred_element_type=F32))
    x1 = x_ref[...] + gta_ref[...] * mixo.reshape(bb, rt, D_MODEL)
    x1_ref[...] = x1
    ms = jnp.mean(x1 * x1, axis=-1, keepdims=True)
    h = x1 * lax.rsqrt(ms + EPS) * gn_ref[...]
    h = h * (1.0 + scf_ref[...]) + shf_ref[...]
    hb = h.reshape(m, D_MODEL).astype(BF16)
    h2_ref[...] = hb.reshape(bb, rt, D_MODEL)

    logit = jnp.dot(hb, wr_ref[...], preferred_element_type=F32) + br_ref[...]
    gl = logit[:, 0:N_GROUPS]
    el = logit[:, N_GROUPS:N_GROUPS + N_EXPERTS]
    gmax, gsel = _first_argmax(gl, N_GROUPS)
    p_g = 1.0 / jnp.sum(jnp.exp(gl - gmax), axis=1, keepdims=True)
    ein = jnp.zeros((m, EXPERTS_PER_GROUP), F32)
    for gi in range(N_GROUPS):
        ein = ein + jnp.where(gsel == gi, el[:, gi * EXPERTS_PER_GROUP:(gi + 1) * EXPERTS_PER_GROUP], 0.0)
    e1, i1 = _first_argmax(ein, EXPERTS_PER_GROUP)
    idx4 = lax.broadcasted_iota(jnp.int32, ein.shape, 1)
    e2, i2 = _first_argmax(jnp.where(idx4 == i1, -jnp.inf, ein), EXPERTS_PER_GROUP)
    r2 = jnp.exp(e2 - e1)
    w1 = p_g / (1.0 + r2)
    w2 = p_g * r2 / (1.0 + r2)
    lane = lax.broadcasted_iota(jnp.int32, (m, LANES), 1)
    base = gsel * EXPERTS_PER_GROUP
    gate = jnp.where(lane == base + i1, w1, 0.0) + jnp.where(lane == base + i2, w2, 0.0)
    gate_ref[...] = gate.reshape(bb, rt, LANES)


def _post_call(x, attn, y, mod, w_out, gn, w_r, b_r, *, bb, rt):
    nb, r, d = x.shape
    tspec = lambda n: pl.BlockSpec((bb, rt, n), lambda i, j: (i, j, 0))
    mspec = lambda c: pl.BlockSpec((bb, 1, d), lambda i, j: (i, 0, c))
    full = lambda a, b_: pl.BlockSpec((a, b_), lambda i, j: (0, 0))
    return pl.pallas_call(
        functools.partial(_post_kernel, bb=bb, rt=rt),
        grid=(nb // bb, r // rt),
        in_specs=[tspec(d), tspec(D_ATTN), tspec(D_CONV), mspec(2), mspec(3), mspec(4),
                  full(D_ATTN + D_CONV, d), full(1, d), full(d, LANES), full(1, LANES)],
        out_specs=[tspec(d), tspec(d), tspec(LANES)],
        out_shape=[jax.ShapeDtypeStruct((nb, r, d), F32), jax.ShapeDtypeStruct((nb, r, d), BF16),
                   jax.ShapeDtypeStruct((nb, r, LANES), F32)],
        compiler_params=_cparams(("arbitrary", "arbitrary")),
        name="out_proj_router",
    )(x, attn, y, mod, mod, mod, w_out, gn, w_r, b_r)


def _moe_kernel(h_ref, gate_ref, x1_ref, gtf_ref, wg_ref, wu_ref, wd_ref, o_ref, acc_ref, *, bb, rt):
    e = pl.program_id(2)
    m = bb * rt

    @pl.when(e == 0)
    def _():
        acc_ref[...] = jnp.zeros(acc_ref.shape, F32)

    hb = h_ref[...].reshape(m, D_MODEL)
    gate = gate_ref[...].reshape(m, LANES)
    lane = lax.broadcasted_iota(jnp.int32, (m, LANES), 1)
    ge = jnp.sum(jnp.where(lane == e, gate, 0.0), axis=1, keepdims=True)
    a = jnp.dot(hb, wg_ref[0], preferred_element_type=F32)
    u = jnp.dot(hb, wu_ref[0], preferred_element_type=F32)
    hid = (a * jax.nn.sigmoid(a)) * u * ge
    acc_ref[...] += jnp.dot(hid.astype(BF16), wd_ref[0], preferred_element_type=F32)

    @pl.when(e == N_EXPERTS - 1)
    def _():
        o_ref[...] = x1_ref[...] + gtf_ref[...] * acc_ref[...].reshape(bb, rt, D_MODEL)


def _moe_call(h2, gate, x1, mod, wg, wu, wd, *, bb, rt):
    nb, r, d = x1.shape
    tspec = lambda n: pl.BlockSpec((bb, rt, n), lambda i, j, e: (i, j, 0))
    return pl.pallas_call(
        functools.partial(_moe_kernel, bb=bb, rt=rt),
        grid=(nb // bb, r // rt, N_EXPERTS),
        in_specs=[tspec(d), tspec(LANES), tspec(d),
                  pl.BlockSpec((bb, 1, d), lambda i, j, e: (i, 0, 5)),
                  pl.BlockSpec((1, d, D_EXPERT), lambda i, j, e: (e, 0, 0)),
                  pl.BlockSpec((1, d, D_EXPERT), lambda i, j, e: (e, 0, 0)),
                  pl.BlockSpec((1, D_EXPERT, d), lambda i, j, e: (e, 0, 0))],
        out_specs=tspec(d),
        out_shape=jax.ShapeDtypeStruct((nb, r, d), F32),
        scratch_shapes=[pltpu.VMEM((bb * rt, d), F32)],
        compiler_params=_cparams(("arbitrary", "arbitrary", "arbitrary")),
        name="moe",
    )(h2, gate, x1, mod, wg, wu, wd)


def _rope_tables(pos):
    half = ROPE_DIM // 2
    inv = ROPE_THETA ** (-jnp.arange(half, dtype=F32) / half)
    ang = pos.astype(F32)[:, None] * inv[None, :]
    cos, sin = jnp.cos(ang), jnp.sin(ang)
    r = pos.shape[0]
    z = lambda n: jnp.zeros((r, n), F32)
    c64 = jnp.concatenate([cos, cos, jnp.ones((r, HEAD_DIM - ROPE_DIM), F32)], axis=1)
    sa64 = jnp.concatenate([-sin, z(HEAD_DIM - half)], axis=1)
    sb64 = jnp.concatenate([z(half), sin, z(HEAD_DIM - ROPE_DIM)], axis=1)
    iw_scale = jnp.full((r, N_IDX_HEADS), N_IDX_HEADS ** -0.5 * IDX_DIM ** -0.5, F32)
    ci = jnp.concatenate([c64, iw_scale, z(LANES - IDX_DIM - N_IDX_HEADS)], axis=1)
    two = lambda a: jnp.concatenate([a, a], axis=1)
    pad = lambda a: jnp.concatenate([a, z(LANES - HEAD_DIM)], axis=1)
    return jnp.stack([two(c64), two(sa64), two(sb64), ci, pad(sa64), pad(sb64)])


def _layer_weights(l, w_in, q_norm_g, k_norm_g, conv_w, w_out, w_router_group, b_router_group,
                   w_router_expert, b_router_expert, w_gate, w_up, w_down):
    d = D_MODEL
    w_cat = jnp.concatenate(
        [w_in[l][:, :IN_SPLIT], jnp.zeros((d, COL_A - IN_SPLIT), F32), w_in[l][:, IN_SPLIT:]], axis=1).astype(BF16)
    gqk = jnp.stack([jnp.tile(q_norm_g[l], 2), jnp.tile(k_norm_g[l], 2)])
    w_r = jnp.concatenate([w_router_group[l], w_router_expert[l],
                           jnp.zeros((d, LANES - N_GROUPS - N_EXPERTS), F32)], axis=1).astype(BF16)
    b_r = jnp.concatenate([b_router_group[l], b_router_expert[l],
                           jnp.zeros((LANES - N_GROUPS - N_EXPERTS,), F32)]).reshape(1, LANES)
    return dict(w_cat=w_cat, gqk=gqk, w_out=w_out[l].astype(BF16), w_r=w_r, b_r=b_r,
                wg=w_gate[l].astype(BF16), wu=w_up[l].astype(BF16), wd=w_down[l].astype(BF16))


def _trunk_layer(x, mod, tabs, past, lw, gn_mix, gn_ffn, conv_w, conv_b, ln_g, ln_b, consts,
                 *, bb, rt, tq, tt, mbb, mrt):
    nb, r, d = x.shape
    bd, tri = consts
    q, iq, kf, vf, kb, va, ikwf, ikwb, g = _inproj_call(
        x, mod, gn_mix.reshape(1, d), lw["w_cat"], lw["gqk"], tabs, bd, bb=bb, rt=rt, tab_blocked=past is None)
    if past is None:
        k_all, va_all, ikb_all, n_keys = kb, va, ikwb, r
        past32 = jnp.zeros((nb, CONV_HALO, D_CONV), F32)
        padded_tail = g[:, r - (CONV_W - 1):]
    else:
        k_past, v_past, ik_past, conv_past = past
        past_len = k_past.shape[1]
        n_keys = past_len + r
        s_pad = -(-n_keys // KEY_TILE) * KEY_TILE
        tail = s_pad - n_keys
        ones = jnp.ones((nb, past_len, N_KV_HEADS, HEAD_DIM), BF16)
        va_past = jnp.concatenate([v_past.astype(BF16), ones], axis=-1).reshape(nb, past_len, 2 * LANES)
        ikw_past = jnp.concatenate([ik_past.astype(BF16), jnp.zeros((nb, past_len, LANES - IDX_DIM), BF16)], axis=-1)
        zpad = lambda n: jnp.zeros((nb, tail, n), BF16)
        k_all = jnp.concatenate([k_past.reshape(nb, past_len, LANES).astype(BF16), kb, zpad(LANES)], axis=1)
        va_all = jnp.concatenate([va_past, va, zpad(2 * LANES)], axis=1)
        ikb_all = jnp.concatenate([ikw_past, ikwb, zpad(LANES)], axis=1)
        past32 = jnp.concatenate([jnp.zeros((nb, CONV_HALO - (CONV_W - 1), D_CONV), F32), conv_past], axis=1)
        padded_tail = jnp.concatenate([conv_past, g], axis=1)[:, -(CONV_W - 1):]
    attn = _attn_call(q, iq, ikwf, k_all, va_all, ikb_all, tri, tq=tq, n_keys=n_keys, causal=past is None)
    y = _conv_call(g, past32, conv_w, conv_b.reshape(1, -1), ln_g.reshape(1, -1), ln_b.reshape(1, -1), tt=tt)
    x1, h2, gate = _post_call(x, attn, y, mod, lw["w_out"], gn_ffn.reshape(1, d), lw["w_r"], lw["b_r"], bb=bb, rt=rt)
    x2 = _moe_call(h2, gate, x1, mod, lw["wg"], lw["wu"], lw["wd"], bb=mbb, rt=mrt)
    new_k = kf.reshape(nb, r, N_KV_HEADS, HEAD_DIM)
    new_v = vf.reshape(nb, r, N_KV_HEADS, HEAD_DIM)
    return x2, (new_k, new_v, ikwf[:, :, :IDX_DIM], padded_tail)


def kernel(x_prompt, x_sample, c_prompt, c_sample, cache_k, cache_v, cache_idx_k, state_conv, w_ada, b_ada, g_norm_mix, w_in, q_norm_g, k_norm_g, conv_w, conv_b, ln_conv_g, ln_conv_b, w_out, g_norm_ffn, w_router_group, b_router_group, w_router_expert, b_router_expert, w_gate, w_up, w_down):
    depth = w_ada.shape[0]
    bp, seq, d = x_prompt.shape
    bs, dseq, _ = x_sample.shape
    past_len = cache_k.shape[2]

    n_c = bp + bs
    n_cp = -(-n_c // SUBLANES) * SUBLANES
    c_all = jnp.concatenate([c_prompt, c_sample, jnp.zeros((n_cp - n_c, d), F32)], axis=0)
    mod_all = _ada_call(c_all, w_ada, b_ada)

    blk = np.arange(LANES) // HEAD_DIM
    bd = jnp.asarray(blk[:, None] == blk[None, :], BF16)
    tri = jnp.asarray(np.arange(LANES)[:, None] <= np.arange(LANES)[None, :], BF16)
    consts = (bd, tri)

    tabs_p = _rope_tables(jnp.arange(seq, dtype=jnp.int32))
    tabs_s = _rope_tables(past_len + jnp.arange(dseq, dtype=jnp.int32))

    xp, xs = x_prompt, x_sample
    st_p, st_s = [], []
    for l in range(depth):
        lw = _layer_weights(l, w_in, q_norm_g, k_norm_g, conv_w, w_out, w_router_group, b_router_group,
                            w_router_expert, b_router_expert, w_gate, w_up, w_down)
        common = (lw, g_norm_mix[l], g_norm_ffn[l], conv_w[l], conv_b[l], ln_conv_g[l], ln_conv_b[l], consts)
        mod_p = mod_all[l, :bp].reshape(bp, 1, 6 * d)
        mod_s = mod_all[l, bp:bp + bs].reshape(bs, 1, 6 * d)
        xp, st = _trunk_layer(xp, mod_p, tabs_p, None, *common,
                              bb=1, rt=512, tq=128, tt=256, mbb=1, mrt=1024)
        st_p.append(st)
        past = (cache_k[l], cache_v[l], cache_idx_k[l], state_conv[l])
        xs, st = _trunk_layer(xs, mod_s, tabs_s, past, *common,
                              bb=8, rt=dseq, tq=dseq, tt=dseq, mbb=16, mrt=dseq)
        st_s.append(st)

    stack = lambda sts, i: jnp.stack([s[i] for s in sts])
    return (xp, xs, stack(st_p, 0), stack(st_p, 1), stack(st_p, 2), stack(st_p, 3),
            stack(st_s, 0), stack(st_s, 1), stack(st_s, 2), stack(st_s, 3))
```

```python
import functools

import jax
import jax.numpy as jnp
import numpy as np
from jax import lax
from jax.experimental import pallas as pl
from jax.experimental.pallas import tpu as pltpu

D_MODEL = 1024
CHUNK = 64
N_HEADS = 8
HEAD_DIM = 64
N_KV_HEADS = 2
KV_GROUP = N_HEADS // N_KV_HEADS
ROPE_DIM = HEAD_DIM // 4
ROPE_THETA = 500000.0
N_IDX_HEADS = 8
IDX_DIM = 64
TOPK_MAX = 256
D_ATTN = N_HEADS * HEAD_DIM
D_CONV = 512
CONV_W = 31
N_GROUPS = 4
EXPERTS_PER_GROUP = 4
N_EXPERTS = N_GROUPS * EXPERTS_PER_GROUP
D_EXPERT = 256
EPS = 1e-6

LANES = 128
SUBLANES = 8
KEY_TILE = 256
NEG_BIAS = -1e30
LOWEST = -3.0e38
SEL_MARK = 3.0e38
MAX_SAFE_SHIFT = 40.0
BOUND_SLACK = 1.01
INT_MIN = -(2 ** 31)
WORD_BITS = 32
SEARCH_CHUNKS = 2
RADIX_STEPS = (3,) * 10 + (2,)
VMEM_LIMIT = 48 * 1024 * 1024

COL_Q = 0
COL_K = 512
COL_V = 640
COL_IQ = 768
COL_IKW = 1280
COL_A = 1408
COL_B = 1920
W_CAT_COLS = 2432
IN_SPLIT = 1352

F32 = jnp.float32
BF16 = jnp.bfloat16
NT_DIMS = (((1,), (1,)), ((), ()))


def _cparams(sem):
    return pltpu.CompilerParams(dimension_semantics=sem, vmem_limit_bytes=VMEM_LIMIT)


def _ada_kernel(c_ref, w_ref, b_ref, o_ref):
    c = c_ref[...]
    s = (c * jax.nn.sigmoid(c)).astype(BF16)
    o_ref[0] = jnp.dot(s, w_ref[0].astype(BF16), preferred_element_type=F32) + b_ref[0]


def _ada_call(c_all, w_ada, b_ada):
    depth, d, n6 = w_ada.shape
    bp = c_all.shape[0]
    tn = 1536
    return pl.pallas_call(
        _ada_kernel,
        grid=(depth, n6 // tn),
        in_specs=[
            pl.BlockSpec((bp, d), lambda l, j: (0, 0)),
            pl.BlockSpec((1, d, tn), lambda l, j: (l, 0, j)),
            pl.BlockSpec((1, 1, tn), lambda l, j: (l, 0, j)),
        ],
        out_specs=pl.BlockSpec((1, bp, tn), lambda l, j: (l, 0, j)),
        out_shape=jax.ShapeDtypeStruct((depth, bp, n6), F32),
        compiler_params=_cparams(("arbitrary", "arbitrary")),
        name="ada_mod",
    )(c_all, w_ada, b_ada.reshape(depth, 1, n6))


def _head_mean_sq(y, bd):
    sq = y * y
    hi = sq.astype(BF16)
    lo = (sq - hi.astype(F32)).astype(BF16)
    s = jnp.dot(hi, bd, preferred_element_type=F32) + jnp.dot(lo, bd, preferred_element_type=F32)
    return s * (1.0 / HEAD_DIM)


def _rope(y, c, sa, sb):
    up = pltpu.roll(y, LANES - ROPE_DIM // 2, 1)
    dn = pltpu.roll(y, ROPE_DIM // 2, 1)
    return y * c + up * sa + dn * sb


def _inproj_kernel(x_ref, sh_ref, sc_ref, gn_ref, w_ref, gqk_ref, tabs_ref, bd_ref,
                   q_ref, iq_ref, kf_ref, vf_ref, kb_ref, va_ref, ikwf_ref, ikwb_ref, g_ref,
                   *, bb, rt):
    m = bb * rt
    x = x_ref[...]
    ms = jnp.mean(x * x, axis=-1, keepdims=True)
    h = x * lax.rsqrt(ms + EPS) * gn_ref[...]
    h = h * (1.0 + sc_ref[...]) + sh_ref[...]
    hb = h.reshape(m, D_MODEL).astype(BF16)

    def tab(i):
        t = tabs_ref[i]
        if bb > 1:
            t = jnp.broadcast_to(t[None], (bb, rt, LANES)).reshape(m, LANES)
        return t

    cq, saq, sbq, ci, sai, sbi = (tab(i) for i in range(6))
    bd = bd_ref[...]
    gq = gqk_ref[0:1, :]
    gk = gqk_ref[1:2, :]

    def proj(c0, n):
        return jnp.dot(hb, w_ref[:, c0:c0 + n], preferred_element_type=F32)

    def put(ref, c0, val):
        ref[:, :, c0:c0 + val.shape[-1]] = val.reshape(bb, rt, val.shape[-1]).astype(ref.dtype)

    pq = proj(COL_Q, D_ATTN)
    for j in range(D_ATTN // LANES):
        y = pq[:, j * LANES:(j + 1) * LANES]
        y = y * lax.rsqrt(_head_mean_sq(y, bd) + EPS) * gq
        put(q_ref, j * LANES, _rope(y, cq, saq, sbq) * (HEAD_DIM ** -0.5))

    pkv = proj(COL_K, 2 * LANES)
    y = pkv[:, :LANES]
    y = y * lax.rsqrt(_head_mean_sq(y, bd) + EPS) * gk
    kk = _rope(y, cq, saq, sbq)
    put(kf_ref, 0, kk)
    put(kb_ref, 0, kk)
    vv = pkv[:, LANES:]
    put(vf_ref, 0, vv)
    lane = lax.broadcasted_iota(jnp.int32, (m, LANES), 1)
    put(va_ref, 0, jnp.where(lane < HEAD_DIM, vv, 1.0))
    put(va_ref, LANES, jnp.where(lane < HEAD_DIM, pltpu.roll(vv, HEAD_DIM, 1), 1.0))

    piq = proj(COL_IQ, N_IDX_HEADS * IDX_DIM)
    for j in range(N_IDX_HEADS * IDX_DIM // LANES):
        put(iq_ref, j * LANES, _rope(piq[:, j * LANES:(j + 1) * LANES], cq, saq, sbq))

    ikw = _rope(proj(COL_IKW, LANES), ci, sai, sbi)
    put(ikwf_ref, 0, ikw)
    put(ikwb_ref, 0, ikw)

    pu = proj(COL_A, 2 * D_CONV)
    put(g_ref, 0, pu[:, :D_CONV] * jax.nn.sigmoid(pu[:, D_CONV:]))


def _inproj_call(x, mod, gn, w_cat, gqk, tabs, bd, *, bb, rt, tab_blocked):
    nb, r, d = x.shape
    grid = (nb // bb, r // rt)
    tok = lambda n, dt: jax.ShapeDtypeStruct((nb, r, n), dt)
    tspec = lambda n: pl.BlockSpec((bb, rt, n), lambda i, j: (i, j, 0))
    return pl.pallas_call(
        functools.partial(_inproj_kernel, bb=bb, rt=rt),
        grid=grid,
        in_specs=[
            tspec(d),
            pl.BlockSpec((bb, 1, d), lambda i, j: (i, 0, 0)),
            pl.BlockSpec((bb, 1, d), lambda i, j: (i, 0, 1)),
            pl.BlockSpec((1, d), lambda i, j: (0, 0)),
            pl.BlockSpec((d, W_CAT_COLS), lambda i, j: (0, 0)),
            pl.BlockSpec((2, LANES), lambda i, j: (0, 0)),
            pl.BlockSpec((6, rt, LANES), (lambda i, j: (0, j, 0)) if tab_blocked else (lambda i, j: (0, 0, 0))),
            pl.BlockSpec((LANES, LANES), lambda i, j: (0, 0)),
        ],
        out_specs=[tspec(D_ATTN), tspec(D_ATTN), tspec(LANES), tspec(LANES), tspec(LANES),
                   tspec(2 * LANES), tspec(LANES), tspec(LANES), tspec(D_CONV)],
        out_shape=[tok(D_ATTN, BF16), tok(D_ATTN, BF16), tok(LANES, F32), tok(LANES, F32),
                   tok(LANES, BF16), tok(2 * LANES, BF16), tok(LANES, F32), tok(LANES, BF16),
                   tok(D_CONV, F32)],
        compiler_params=_cparams(("arbitrary", "arbitrary")),
        name="in_proj",
    )(x, mod, mod, gn, w_cat, gqk, tabs, bd)


def _stack_heads(ref, tq, heads):
    lane = lax.broadcasted_iota(jnp.int32, (tq, LANES), 1)
    low = lane < HEAD_DIM
    out = []
    for h, half in heads:
        slab = ref[0, :, (h // 2) * LANES:(h // 2 + 1) * LANES].astype(F32)
        if (h % 2) != half:
            slab = pltpu.roll(slab, HEAD_DIM, 1)
        keep = low if half == 0 else jnp.logical_not(low)
        out.append(jnp.where(keep, slab, 0.0).astype(BF16))
    return jnp.concatenate(out, axis=0)


def _tile_loop(nt, body, init):
    def run(t0, n, carry):
        for u in range(n):
            carry = body(t0 + u, carry)
        return carry

    if isinstance(nt, int):
        carry = lax.fori_loop(0, nt // 4, lambda p, c: run(4 * p, 4, c), init)
        return run(nt - nt % 4, nt % 4, carry)
    carry = lax.fori_loop(0, lax.shift_right_logical(nt, 2), lambda p, c: run(4 * p, 4, c), init)
    t2 = nt & ~3
    carry = lax.cond((nt & 2) == 2, lambda c: run(t2, 2, c), lambda c: c, carry)
    t1 = nt & ~1
    return lax.cond((nt & 1) == 1, lambda c: run(t1, 1, c), lambda c: c, carry)


def _ukey(s):
    b = lax.bitcast_convert_type(s, jnp.int32)
    return b ^ (lax.shift_right_arithmetic(b, WORD_BITS - 1) | jnp.int32(INT_MIN))


def _bit_transpose(a):
    a = list(a)
    for j, mask in ((16, 0x0000FFFF), (8, 0x00FF00FF), (4, 0x0F0F0F0F), (2, 0x33333333), (1, 0x55555555)):
        k = 0
        while k < WORD_BITS:
            t = (a[k] ^ lax.shift_right_logical(a[k + j], j)) & jnp.int32(mask)
            a[k] = a[k] ^ t
            a[k + j] = a[k + j] ^ lax.shift_left(t, j)
            k = (k + j + 1) & ~j
    return a


def _attn_kernel(q_ref, iq_ref, ikwf_ref, k_ref, va_ref, ikb_ref, tri_ref, o_ref,
                 sc_ref, m_ref, acc_ref, kn_ref, st_ref, pa_ref, *pb_refs, tq, n_keys, causal, ktop):
    i = pl.program_id(1)
    sub = KEY_TILE // LANES
    if causal:
        nt = lax.shift_right_logical(i * tq + tq + KEY_TILE - 1, 8)
    else:
        nt = (n_keys + KEY_TILE - 1) // KEY_TILE
    rows = lax.broadcasted_iota(jnp.int32, (tq, LANES), 0)
    lanes = lax.broadcasted_iota(jnp.int32, (tq, LANES), 1)

    iqs = _stack_heads(iq_ref, tq, [(h, 0) for h in range(N_IDX_HEADS)])
    w8 = ikwf_ref[0][:, IDX_DIM:IDX_DIM + N_IDX_HEADS]
    wb = [jnp.broadcast_to(w8[:, h:h + 1], (tq, LANES)) for h in range(N_IDX_HEADS)]

    def score_body(t, carry):
        kt = ikb_ref[0, pl.ds(pl.multiple_of(t * KEY_TILE, KEY_TILE), KEY_TILE), :]
        d = lax.dot_general(iqs, kt, NT_DIMS, preferred_element_type=F32)
        for c in range(sub):
            s = jnp.zeros((tq, LANES), F32)
            for h in range(N_IDX_HEADS):
                s = s + jnp.maximum(d[h * tq:(h + 1) * tq, c * LANES:(c + 1) * LANES], 0.0) * wb[h]
            kpos = t * KEY_TILE + c * LANES + lanes
            if causal:
                adm = lax.shift_right_logical(kpos, 6) <= lax.shift_right_logical(i * tq + rows, 6)
            else:
                adm = kpos < n_keys
            sc_ref[t * sub + c] = jnp.where(adm, jnp.where(s == 0.0, 0.0, s), -jnp.inf)
        return carry

    _tile_loop(nt, score_body, 0)

    def fill_body(j, carry):
        sc_ref[j] = jnp.full((tq, LANES), -jnp.inf, F32)
        return carry

    lax.fori_loop(nt * sub, WORD_BITS, fill_body, 0)

    def planes_body(r, carry):
        rs = pl.ds(pl.multiple_of(r * SUBLANES, SUBLANES), SUBLANES)
        words = _bit_transpose([_ukey(sc_ref[j, rs, :]) for j in range(WORD_BITS)])
        for p in range(WORD_BITS):
            pa_ref[p, rs, :] = words[p]
        for e, pb_ref in enumerate(pb_refs):
            u = _ukey(sc_ref[WORD_BITS + e, rs, :])
            for p in range(WORD_BITS):
                pb_ref[p, rs, :] = lax.shift_right_logical(u, WORD_BITS - 1 - p) & 1
        return carry

    lax.fori_loop(0, tq // SUBLANES, planes_body, 0)

    plane_refs = (pa_ref,) + tuple(pb_refs)
    ones_m = jnp.ones((LANES, LANES), BF16)

    rc = tq // SEARCH_CHUNKS

    def lane_total(parts):
        x = jnp.concatenate([p.astype(F32).astype(BF16) for p in parts], axis=0)
        tot = jnp.dot(x, ones_m, preferred_element_type=F32)
        return [tot[n * rc:(n + 1) * rc] for n in range(len(parts))]

    def popcount_sum(words):
        return sum(lax.population_count(w) for w in words)

    eqs = [[jnp.full((rc, LANES), -1, jnp.int32) for _ in plane_refs] for _ in range(SEARCH_CHUNKS)]
    cnts = [jnp.zeros((rc, LANES), F32) for _ in range(SEARCH_CHUNKS)]
    ukeys = [jnp.zeros((rc, LANES), jnp.int32) for _ in range(SEARCH_CHUNKS)]
    plane0 = 0
    for nbits in RADIX_STEPS:
        for ch in range(SEARCH_CHUNKS):
            r0 = ch * rc
            leaves = [list(eqs[ch])]
            for b in range(nbits):
                planes = [pref[plane0 + b, r0:r0 + rc, :] for pref in plane_refs]
                split = []
                for words in leaves:
                    ones = [w & p for w, p in zip(words, planes)]
                    split += [ones, [w ^ o for w, o in zip(words, ones)]]
                leaves = split
            totals = lane_total([popcount_sum(words) for words in leaves[:-1]])
            seen = [cnts[ch]]
            for tot in totals:
                seen.append(seen[-1] + tot)
            decided = [s >= ktop for s in seen[1:]]
            new_eq, new_cnt, digit = leaves[-1], seen[-1], 0
            for n in reversed(range(len(decided))):
                new_eq = [jnp.where(decided[n], a, c) for a, c in zip(leaves[n], new_eq)]
                new_cnt = jnp.where(decided[n], seen[n], new_cnt)
                digit = jnp.where(decided[n], len(decided) - n, digit)
            eqs[ch], cnts[ch] = new_eq, new_cnt
            ukeys[ch] = ukeys[ch] | lax.shift_left(digit, WORD_BITS - plane0 - nbits)
        plane0 += nbits
    ukey = jnp.concatenate(ukeys, axis=0)

    bits = jnp.where(ukey < 0, ukey ^ jnp.int32(INT_MIN), ~ukey)
    guess = lax.bitcast_convert_type(bits, F32)[:, 0:1]
    guess = jnp.where(guess == -jnp.inf, LOWEST, guess)

    def counts(thr):
        thr_b = jnp.broadcast_to(thr, (tq, LANES))

        def body(t, acc):
            gt, ge = acc
            for c in range(sub):
                s = sc_ref[t * sub + c]
                gt = gt + jnp.where(s > thr_b, 1.0, 0.0)
                ge = ge + jnp.where(s >= thr_b, 1.0, 0.0)
            return gt, ge

        zero = jnp.zeros((tq, LANES), F32)
        gt, ge = _tile_loop(nt, body, (zero, zero))
        return jnp.sum(gt, axis=1, keepdims=True), jnp.sum(ge, axis=1, keepdims=True)

    def keep(thr, n_gt, n_ge):
        for n, v in enumerate((thr, n_gt, n_ge)):
            st_ref[n] = jnp.broadcast_to(v, (tq, LANES))

    n_gt0, n_ge0 = counts(guess)
    is_kth = ((n_gt0 < ktop) & (n_ge0 >= ktop)) | ((guess == LOWEST) & (n_ge0 < ktop))
    confirmed = jnp.min(jnp.where(is_kth, 1.0, 0.0)) > 0.5
    keep(guess, n_gt0, n_ge0)

    @pl.when(jnp.logical_not(confirmed))
    def _():
        def key_to_float(key):
            return lax.bitcast_convert_type(jnp.where(key >= 0, key, key ^ jnp.int32(0x7FFFFFFF)), F32)

        def bit_body(b, key):
            cand = key + lax.shift_left(jnp.int32(1), WORD_BITS - 1 - b)
            return jnp.where(counts(key_to_float(cand))[1] >= ktop, cand, key)

        key = lax.fori_loop(0, WORD_BITS, bit_body, jnp.full((tq, 1), INT_MIN, jnp.int32))
        thr = jnp.where(key == INT_MIN, LOWEST, key_to_float(key))
        keep(thr, *counts(thr))

    thr_b = st_ref[0]
    n_gt = st_ref[1]
    has_ties = jnp.max(jnp.where(thr_b == LOWEST, 0.0, st_ref[2])) > ktop

    @pl.when(has_ties)
    def _():
        tri = tri_ref[...]
        need_b = jnp.broadcast_to(ktop - n_gt, (tq, LANES))

        def tie_body(j, seen):
            s = sc_ref[j]
            eq = s == thr_b
            pre = jnp.dot(jnp.where(eq, 1.0, 0.0).astype(BF16), tri, preferred_element_type=F32)
            sel = (s > thr_b) | (eq & (pre + seen <= need_b))
            sc_ref[j] = jnp.where(sel, SEL_MARK, -SEL_MARK)
            return seen + jnp.broadcast_to(pre[:, LANES - 1:LANES], (tq, LANES))

        lax.fori_loop(0, nt * sub, tie_body, jnp.zeros((tq, LANES), F32))

    sel_thr = jnp.where(has_ties, 0.0, thr_b)

    def bias(t, c):
        return jnp.where(sc_ref[t * sub + c] >= sel_thr, 0.0, NEG_BIAS)

    qs = [_stack_heads(q_ref, tq, [(g * KV_GROUP + hh, g) for hh in range(KV_GROUP)])
          for g in range(N_KV_HEADS)]
    acc_ref[...] = jnp.zeros(acc_ref.shape, F32)
    sq_rows = lax.broadcasted_iota(jnp.int32, (LANES, LANES), 0)
    sq_lanes = lax.broadcasted_iota(jnp.int32, (LANES, LANES), 1)
    head_block = (lax.shift_right_logical(sq_rows, 6) == lax.shift_right_logical(sq_lanes, 6)).astype(BF16)

    def split_dot(x, w):
        hi = x.astype(BF16)
        lo = (x - hi.astype(F32)).astype(BF16)
        return jnp.dot(hi, w, preferred_element_type=F32) + jnp.dot(lo, w, preferred_element_type=F32)

    @pl.when(i == 0)
    def _():
        def kn_body(t, mx):
            kt = k_ref[0, pl.ds(pl.multiple_of(t * KEY_TILE, KEY_TILE), KEY_TILE), :].astype(F32)
            return jnp.maximum(mx, jnp.dot((kt * kt).astype(BF16), head_block, preferred_element_type=F32))

        mx = _tile_loop(k_ref.shape[1] // KEY_TILE, kn_body, jnp.zeros((KEY_TILE, LANES), F32))
        kn_ref[...] = jnp.broadcast_to(jnp.max(mx, axis=0, keepdims=True), kn_ref.shape)

    kn = kn_ref[0:1, :]
    kn_other = pltpu.roll(kn, HEAD_DIM, 1)
    low_half = lanes[0:1, :] < HEAD_DIM
    bounds = []
    for g in range(N_KV_HEADS):
        qf = qs[g].astype(F32)
        kn_g = jnp.where(low_half, kn, kn_other) if g == 0 else jnp.where(low_half, kn_other, kn)
        bounds.append(jnp.sqrt(split_dot(qf * qf, ones_m) * kn_g) * BOUND_SLACK)
    bounded = jnp.maximum(jnp.max(bounds[0]), jnp.max(bounds[1])) < MAX_SAFE_SHIFT

    def logits(t, g):
        kt = k_ref[0, pl.ds(pl.multiple_of(t * KEY_TILE, KEY_TILE), KEY_TILE), :]
        return lax.dot_general(qs[g], kt, NT_DIMS, preferred_element_type=F32)

    @pl.when(bounded)
    def _():
        for g in range(N_KV_HEADS):
            m_ref[g] = bounds[g]

    @pl.when(jnp.logical_not(bounded))
    def _():
        m_ref[...] = jnp.full(m_ref.shape, NEG_BIAS, F32)

        def max_body(t, carry):
            bs = [bias(t, c) for c in range(sub)]
            for g in range(N_KV_HEADS):
                lg = logits(t, g)
                for hh in range(KV_GROUP):
                    r0 = hh * tq
                    mm = m_ref[g, r0:r0 + tq]
                    for c in range(sub):
                        mm = jnp.maximum(mm, lg[r0:r0 + tq, c * LANES:(c + 1) * LANES] + bs[c])
                    m_ref[g, r0:r0 + tq] = mm
            return carry

        _tile_loop(nt, max_body, 0)
        for g in range(N_KV_HEADS):
            mrow = jnp.max(m_ref[g], axis=1, keepdims=True)
            m_ref[g] = jnp.broadcast_to(mrow, m_ref.shape[1:])

    def pv_body(t, carry):
        bs = [bias(t, c) for c in range(sub)]
        for g in range(N_KV_HEADS):
            lg = logits(t, g)
            ps = []
            for hh in range(KV_GROUP):
                r0 = hh * tq
                mm = m_ref[g, r0:r0 + tq]
                ps.append(jnp.concatenate(
                    [jnp.exp(lg[r0:r0 + tq, c * LANES:(c + 1) * LANES] + bs[c] - mm)
                     for c in range(sub)], axis=1).astype(BF16))
            p = jnp.concatenate(ps, axis=0)
            va = va_ref[0, pl.ds(pl.multiple_of(t * KEY_TILE, KEY_TILE), KEY_TILE), g * LANES:(g + 1) * LANES]
            acc_ref[g] += jnp.dot(p, va, preferred_element_type=F32)
        return carry

    _tile_loop(nt, pv_body, 0)

    outs = []
    for h in range(N_HEADS):
        g, hh = divmod(h, KV_GROUP)
        a = acc_ref[g, hh * tq:(hh + 1) * tq]
        outs.append(a[:, :HEAD_DIM] / a[:, HEAD_DIM:HEAD_DIM + 1])
    o_ref[0] = jnp.concatenate(outs, axis=1).astype(o_ref.dtype)


def _attn_call(q, iq, ikwf, k_all, va_all, ikb_all, tri, *, tq, n_keys, causal):
    b, t, _ = q.shape
    s_pad = k_all.shape[1]
    ktop = min(TOPK_MAX, n_keys // 4)
    n_extra = max(-(-n_keys // LANES) - WORD_BITS, 0)
    qspec = lambda n: pl.BlockSpec((1, tq, n), lambda bi, i: (bi, i, 0))
    kspec = lambda n: pl.BlockSpec((1, s_pad, n), lambda bi, i: (bi, 0, 0))
    return pl.pallas_call(
        functools.partial(_attn_kernel, tq=tq, n_keys=n_keys, causal=causal, ktop=ktop),
        grid=(b, t // tq),
        in_specs=[qspec(D_ATTN), qspec(D_ATTN), qspec(LANES), kspec(LANES), kspec(2 * LANES),
                  kspec(LANES), pl.BlockSpec((LANES, LANES), lambda bi, i: (0, 0))],
        out_specs=qspec(D_ATTN),
        out_shape=jax.ShapeDtypeStruct((b, t, D_ATTN), BF16),
        scratch_shapes=[
            pltpu.VMEM((max(s_pad // LANES, WORD_BITS), tq, LANES), F32),
            pltpu.VMEM((N_KV_HEADS, KV_GROUP * tq, LANES), F32),
            pltpu.VMEM((N_KV_HEADS, KV_GROUP * tq, LANES), F32),
            pltpu.VMEM((SUBLANES, LANES), F32),
            pltpu.VMEM((3, tq, LANES), F32),
        ] + [pltpu.VMEM((WORD_BITS, tq, LANES), jnp.int32)] * (1 + n_extra),
        compiler_params=_cparams(("arbitrary", "arbitrary")),
        name="dsa_attention",
    )(q, iq, ikwf, k_all, va_all, ikb_all, tri)


CONV_HALO = 32
CONV_ROWS = 32


def _conv_kernel(g_ref, past_ref, w_ref, b_ref, lg_ref, lb_ref, y_ref, win_ref, *, tt):
    j = pl.program_id(1)

    @pl.when(j == 0)
    def _():
        win_ref[0, 0:CONV_HALO] = past_ref[0]

    @pl.when(j > 0)
    def _():
        win_ref[0, 0:CONV_HALO] = win_ref[0, tt:tt + CONV_HALO]

    win_ref[0, CONV_HALO:CONV_HALO + tt] = g_ref[0]
    rows = tt + CONV_HALO - SUBLANES
    for b in range(1, SUBLANES):
        win_ref[b, 0:rows] = win_ref[0, b:b + rows]
    off = CONV_HALO - (CONV_W - 1)
    bias = b_ref[...]
    lg = lg_ref[...]
    lb = lb_ref[...]
    for r0 in range(0, tt, CONV_ROWS):
        acc = jnp.zeros((CONV_ROWS // SUBLANES, SUBLANES, D_CONV), F32)
        for tap in range(CONV_W):
            a0, b = divmod(tap + off, SUBLANES)
            rows_tap = win_ref[b, r0 + a0 * SUBLANES:r0 + a0 * SUBLANES + CONV_ROWS]
            acc = acc + rows_tap.reshape(acc.shape) * w_ref[tap][None]
        acc = acc.reshape(CONV_ROWS, D_CONV)
        acc = acc + bias
        mu = jnp.mean(acc, axis=-1, keepdims=True)
        xc = acc - mu
        var = jnp.mean(xc * xc, axis=-1, keepdims=True)
        z = xc * lax.rsqrt(var + EPS) * lg + lb
        y_ref[0, r0:r0 + CONV_ROWS] = (z * jax.nn.sigmoid(z)).astype(y_ref.dtype)


def _conv_call(g, past32, conv_w, conv_b, ln_g, ln_b, *, tt):
    nb, r, _ = g.shape
    vec = pl.BlockSpec((1, D_CONV), lambda bi, j: (0, 0))
    return pl.pallas_call(
        functools.partial(_conv_kernel, tt=tt),
        grid=(nb, r // tt),
        in_specs=[
            pl.BlockSpec((1, tt, D_CONV), lambda bi, j: (bi, j, 0)),
            pl.BlockSpec((1, CONV_HALO, D_CONV), lambda bi, j: (bi, 0, 0)),
            pl.BlockSpec((CONV_W, SUBLANES, D_CONV), lambda bi, j: (0, 0, 0)),
            vec, vec, vec,
        ],
        out_specs=pl.BlockSpec((1, tt, D_CONV), lambda bi, j: (bi, j, 0)),
        out_shape=jax.ShapeDtypeStruct((nb, r, D_CONV), BF16),
        scratch_shapes=[pltpu.VMEM((SUBLANES, tt + CONV_HALO, D_CONV), F32)],
        compiler_params=_cparams(("arbitrary", "arbitrary")),
        name="conv_module",
    )(g, past32, jnp.broadcast_to(conv_w[:, None, :], (CONV_W, SUBLANES, D_CONV)), conv_b, ln_g, ln_b)


def _first_argmax(v, width):
    idx = lax.broadcasted_iota(jnp.int32, v.shape, 1)
    mx = jnp.max(v, axis=1, keepdims=True)
    first = jnp.min(jnp.where(v == mx, idx, width), axis=1, keepdims=True)
    return mx, first


def _post_kernel(x_ref, at_ref, y_ref, gta_ref, shf_ref, scf_ref, wo_ref, gn_ref, wr_ref, br_ref,
                 x1_ref, h2_ref, gate_ref, *, bb, rt):
    m = bb * rt
    at = at_ref[...].reshape(m, D_ATTN)
    yy = y_ref[...].reshape(m, D_CONV)
    mixo = (jnp.dot(at, wo_ref[0:D_ATTN], preferred_element_type=F32)
            + jnp.dot(yy, wo_ref[D_ATTN:D_ATTN + D_CONV], preferred_element_type=F32))
    x1 = x_ref[...] + gta_ref[...] * mixo.reshape(bb, rt, D_MODEL)
    x1_ref[...] = x1
    ms = jnp.mean(x1 * x1, axis=-1, keepdims=True)
    h = x1 * lax.rsqrt(ms + EPS) * gn_ref[...]
    h = h * (1.0 + scf_ref[...]) + shf_ref[...]
    hb = h.reshape(m, D_MODEL).astype(BF16)
    h2_ref[...] = hb.reshape(bb, rt, D_MODEL)

    logit = jnp.dot(hb, wr_ref[...], preferred_element_type=F32) + br_ref[...]
    gl = logit[:, 0:N_GROUPS]
    el = logit[:, N_GROUPS:N_GROUPS + N_EXPERTS]
    gmax, gsel = _first_argmax(gl, N_GROUPS)
    p_g = 1.0 / jnp.sum(jnp.exp(gl - gmax), axis=1, keepdims=True)
    ein = jnp.zeros((m, EXPERTS_PER_GROUP), F32)
    for gi in range(N_GROUPS):
        ein = ein + jnp.where(gsel == gi, el[:, gi * EXPERTS_PER_GROUP:(gi + 1) * EXPERTS_PER_GROUP], 0.0)
    e1, i1 = _first_argmax(ein, EXPERTS_PER_GROUP)
    idx4 = lax.broadcasted_iota(jnp.int32, ein.shape, 1)
    e2, i2 = _first_argmax(jnp.where(idx4 == i1, -jnp.inf, ein), EXPERTS_PER_GROUP)
    r2 = jnp.exp(e2 - e1)
    w1 = p_g / (1.0 + r2)
    w2 = p_g * r2 / (1.0 + r2)
    lane = lax.broadcasted_iota(jnp.int32, (m, LANES), 1)
    base = gsel * EXPERTS_PER_GROUP
    gate = jnp.where(lane == base + i1, w1, 0.0) + jnp.where(lane == base + i2, w2, 0.0)
    gate_ref[...] = gate.reshape(bb, rt, LANES)


def _post_call(x, attn, y, mod, w_out, gn, w_r, b_r, *, bb, rt):
    nb, r, d = x.shape
    tspec = lambda n: pl.BlockSpec((bb, rt, n), lambda i, j: (i, j, 0))
    mspec = lambda c: pl.BlockSpec((bb, 1, d), lambda i, j: (i, 0, c))
    full = lambda a, b_: pl.BlockSpec((a, b_), lambda i, j: (0, 0))
    return pl.pallas_call(
        functools.partial(_post_kernel, bb=bb, rt=rt),
        grid=(nb // bb, r // rt),
        in_specs=[tspec(d), tspec(D_ATTN), tspec(D_CONV), mspec(2), mspec(3), mspec(4),
                  full(D_ATTN + D_CONV, d), full(1, d), full(d, LANES), full(1, LANES)],
        out_specs=[tspec(d), tspec(d), tspec(LANES)],
        out_shape=[jax.ShapeDtypeStruct((nb, r, d), F32), jax.ShapeDtypeStruct((nb, r, d), BF16),
                   jax.ShapeDtypeStruct((nb, r, LANES), F32)],
        compiler_params=_cparams(("arbitrary", "arbitrary")),
        name="out_proj_router",
    )(x, attn, y, mod, mod, mod, w_out, gn, w_r, b_r)


def _moe_kernel(h_ref, gate_ref, x1_ref, gtf_ref, wg_ref, wu_ref, wd_ref, o_ref, acc_ref, *, bb, rt):
    e = pl.program_id(2)
    m = bb * rt

    @pl.when(e == 0)
    def _():
        acc_ref[...] = jnp.zeros(acc_ref.shape, F32)

    hb = h_ref[...].reshape(m, D_MODEL)
    gate = gate_ref[...].reshape(m, LANES)
    lane = lax.broadcasted_iota(jnp.int32, (m, LANES), 1)
    ge = jnp.sum(jnp.where(lane == e, gate, 0.0), axis=1, keepdims=True)
    a = jnp.dot(hb, wg_ref[0], preferred_element_type=F32)
    u = jnp.dot(hb, wu_ref[0], preferred_element_type=F32)
    hid = (a * jax.nn.sigmoid(a)) * u * ge
    acc_ref[...] += jnp.dot(hid.astype(BF16), wd_ref[0], preferred_element_type=F32)

    @pl.when(e == N_EXPERTS - 1)
    def _():
        o_ref[...] = x1_ref[...] + gtf_ref[...] * acc_ref[...].reshape(bb, rt, D_MODEL)


def _moe_call(h2, gate, x1, mod, wg, wu, wd, *, bb, rt):
    nb, r, d = x1.shape
    tspec = lambda n: pl.BlockSpec((bb, rt, n), lambda i, j, e: (i, j, 0))
    return pl.pallas_call(
        functools.partial(_moe_kernel, bb=bb, rt=rt),
        grid=(nb // bb, r // rt, N_EXPERTS),
        in_specs=[tspec(d), tspec(LANES), tspec(d),
                  pl.BlockSpec((bb, 1, d), lambda i, j, e: (i, 0, 5)),
                  pl.BlockSpec((1, d, D_EXPERT), lambda i, j, e: (e, 0, 0)),
                  pl.BlockSpec((1, d, D_EXPERT), lambda i, j, e: (e, 0, 0)),
                  pl.BlockSpec((1, D_EXPERT, d), lambda i, j, e: (e, 0, 0))],
        out_specs=tspec(d),
        out_shape=jax.ShapeDtypeStruct((nb, r, d), F32),
        scratch_shapes=[pltpu.VMEM((bb * rt, d), F32)],
        compiler_params=_cparams(("arbitrary", "arbitrary", "arbitrary")),
        name="moe",
    )(h2, gate, x1, mod, wg, wu, wd)


def _rope_tables(pos):
    half = ROPE_DIM // 2
    inv = ROPE_THETA ** (-jnp.arange(half, dtype=F32) / half)
    ang = pos.astype(F32)[:, None] * inv[None, :]
    cos, sin = jnp.cos(ang), jnp.sin(ang)
    r = pos.shape[0]
    z = lambda n: jnp.zeros((r, n), F32)
    c64 = jnp.concatenate([cos, cos, jnp.ones((r, HEAD_DIM - ROPE_DIM), F32)], axis=1)
    sa64 = jnp.concatenate([-sin, z(HEAD_DIM - half)], axis=1)
    sb64 = jnp.concatenate([z(half), sin, z(HEAD_DIM - ROPE_DIM)], axis=1)
    iw_scale = jnp.full((r, N_IDX_HEADS), N_IDX_HEADS ** -0.5 * IDX_DIM ** -0.5, F32)
    ci = jnp.concatenate([c64, iw_scale, z(LANES - IDX_DIM - N_IDX_HEADS)], axis=1)
    two = lambda a: jnp.concatenate([a, a], axis=1)
    pad = lambda a: jnp.concatenate([a, z(LANES - HEAD_DIM)], axis=1)
    return jnp.stack([two(c64), two(sa64), two(sb64), ci, pad(sa64), pad(sb64)])


def _layer_weights(l, w_in, q_norm_g, k_norm_g, conv_w, w_out, w_router_group, b_router_group,
                   w_router_expert, b_router_expert, w_gate, w_up, w_down):
    d = D_MODEL
    w_cat = jnp.concatenate(
        [w_in[l][:, :IN_SPLIT], jnp.zeros((d, COL_A - IN_SPLIT), F32), w_in[l][:, IN_SPLIT:]], axis=1).astype(BF16)
    gqk = jnp.stack([jnp.tile(q_norm_g[l], 2), jnp.tile(k_norm_g[l], 2)])
    w_r = jnp.concatenate([w_router_group[l], w_router_expert[l],
                           jnp.zeros((d, LANES - N_GROUPS - N_EXPERTS), F32)], axis=1).astype(BF16)
    b_r = jnp.concatenate([b_router_group[l], b_router_expert[l],
                           jnp.zeros((LANES - N_GROUPS - N_EXPERTS,), F32)]).reshape(1, LANES)
    return dict(w_cat=w_cat, gqk=gqk, w_out=w_out[l].astype(BF16), w_r=w_r, b_r=b_r,
                wg=w_gate[l].astype(BF16), wu=w_up[l].astype(BF16), wd=w_down[l].astype(BF16))


def _trunk_layer(x, mod, tabs, past, lw, gn_mix, gn_ffn, conv_w, conv_b, ln_g, ln_b, consts,
                 *, bb, rt, tq, tt, mbb, mrt):
    nb, r, d = x.shape
    bd, tri = consts
    q, iq, kf, vf, kb, va, ikwf, ikwb, g = _inproj_call(
        x, mod, gn_mix.reshape(1, d), lw["w_cat"], lw["gqk"], tabs, bd, bb=bb, rt=rt, tab_blocked=past is None)
    if past is None:
        k_all, va_all, ikb_all, n_keys = kb, va, ikwb, r
        past32 = jnp.zeros((nb, CONV_HALO, D_CONV), F32)
        padded_tail = g[:, r - (CONV_W - 1):]
    else:
        k_past, v_past, ik_past, conv_past = past
        past_len = k_past.shape[1]
        n_keys = past_len + r
        s_pad = -(-n_keys // KEY_TILE) * KEY_TILE
        tail = s_pad - n_keys
        ones = jnp.ones((nb, past_len, N_KV_HEADS, HEAD_DIM), BF16)
        va_past = jnp.concatenate([v_past.astype(BF16), ones], axis=-1).reshape(nb, past_len, 2 * LANES)
        ikw_past = jnp.concatenate([ik_past.astype(BF16), jnp.zeros((nb, past_len, LANES - IDX_DIM), BF16)], axis=-1)
        zpad = lambda n: jnp.zeros((nb, tail, n), BF16)
        k_all = jnp.concatenate([k_past.reshape(nb, past_len, LANES).astype(BF16), kb, zpad(LANES)], axis=1)
        va_all = jnp.concatenate([va_past, va, zpad(2 * LANES)], axis=1)
        ikb_all = jnp.concatenate([ikw_past, ikwb, zpad(LANES)], axis=1)
        past32 = jnp.concatenate([jnp.zeros((nb, CONV_HALO - (CONV_W - 1), D_CONV), F32), conv_past], axis=1)
        padded_tail = jnp.concatenate([conv_past, g], axis=1)[:, -(CONV_W - 1):]
    attn = _attn_call(q, iq, ikwf, k_all, va_all, ikb_all, tri, tq=tq, n_keys=n_keys, causal=past is None)
    y = _conv_call(g, past32, conv_w, conv_b.reshape(1, -1), ln_g.reshape(1, -1), ln_b.reshape(1, -1), tt=tt)
    x1, h2, gate = _post_call(x, attn, y, mod, lw["w_out"], gn_ffn.reshape(1, d), lw["w_r"], lw["b_r"], bb=bb, rt=rt)
    x2 = _moe_call(h2, gate, x1, mod, lw["wg"], lw["wu"], lw["wd"], bb=mbb, rt=mrt)
    new_k = kf.reshape(nb, r, N_KV_HEADS, HEAD_DIM)
    new_v = vf.reshape(nb, r, N_KV_HEADS, HEAD_DIM)
    return x2, (new_k, new_v, ikwf[:, :, :IDX_DIM], padded_tail)


def kernel(x_prompt, x_sample, c_prompt, c_sample, cache_k, cache_v, cache_idx_k, state_conv, w_ada, b_ada, g_norm_mix, w_in, q_norm_g, k_norm_g, conv_w, conv_b, ln_conv_g, ln_conv_b, w_out, g_norm_ffn, w_router_group, b_router_group, w_router_expert, b_router_expert, w_gate, w_up, w_down):
    depth = w_ada.shape[0]
    bp, seq, d = x_prompt.shape
    bs, dseq, _ = x_sample.shape
    past_len = cache_k.shape[2]

    n_c = bp + bs
    n_cp = -(-n_c // SUBLANES) * SUBLANES
    c_all = jnp.concatenate([c_prompt, c_sample, jnp.zeros((n_cp - n_c, d), F32)], axis=0)
    mod_all = _ada_call(c_all, w_ada, b_ada)

    blk = np.arange(LANES) // HEAD_DIM
    bd = jnp.asarray(blk[:, None] == blk[None, :], BF16)
    tri = jnp.asarray(np.arange(LANES)[:, None] <= np.arange(LANES)[None, :], BF16)
    consts = (bd, tri)

    tabs_p = _rope_tables(jnp.arange(seq, dtype=jnp.int32))
    tabs_s = _rope_tables(past_len + jnp.arange(dseq, dtype=jnp.int32))

    xp, xs = x_prompt, x_sample
    st_p, st_s = [], []
    for l in range(depth):
        lw = _layer_weights(l, w_in, q_norm_g, k_norm_g, conv_w, w_out, w_router_group, b_router_group,
                            w_router_expert, b_router_expert, w_gate, w_up, w_down)
        common = (lw, g_norm_mix[l], g_norm_ffn[l], conv_w[l], conv_b[l], ln_conv_g[l], ln_conv_b[l], consts)
        mod_p = mod_all[l, :bp].reshape(bp, 1, 6 * d)
        mod_s = mod_all[l, bp:bp + bs].reshape(bs, 1, 6 * d)
        xp, st = _trunk_layer(xp, mod_p, tabs_p, None, *common,
                              bb=1, rt=512, tq=128, tt=256, mbb=1, mrt=1024)
        st_p.append(st)
        past = (cache_k[l], cache_v[l], cache_idx_k[l], state_conv[l])
        xs, st = _trunk_layer(xs, mod_s, tabs_s, past, *common,
                              bb=8, rt=dseq, tq=dseq, tt=dseq, mbb=16, mrt=dseq)
        st_s.append(st)

    stack = lambda sts, i: jnp.stack([s[i] for s in sts])
    return (xp, xs, stack(st_p, 0), stack(st_p, 1), stack(st_p, 2), stack(st_p, 3),
            stack(st_s, 0), stack(st_s, 1), stack(st_s, 2), stack(st_s, 3))
```

```python
import functools

import jax
import jax.numpy as jnp
import numpy as np
from jax import lax
from jax.experimental import pallas as pl
from jax.experimental.pallas import tpu as pltpu

D_MODEL = 1024
CHUNK = 64
N_HEADS = 8
HEAD_DIM = 64
N_KV_HEADS = 2
KV_GROUP = N_HEADS // N_KV_HEADS
ROPE_DIM = HEAD_DIM // 4
ROPE_THETA = 500000.0
N_IDX_HEADS = 8
IDX_DIM = 64
TOPK_MAX = 256
D_ATTN = N_HEADS * HEAD_DIM
D_CONV = 512
CONV_W = 31
N_GROUPS = 4
EXPERTS_PER_GROUP = 4
N_EXPERTS = N_GROUPS * EXPERTS_PER_GROUP
D_EXPERT = 256
EPS = 1e-6

LANES = 128
SUBLANES = 8
KEY_TILE = 256
NEG_BIAS = -1e30
COUNT_UNIT = 64.0
LOWEST = -3.0e38
SEL_MARK = 3.0e38
MAX_SAFE_SHIFT = 40.0
BOUND_SLACK = 1.01
INT_MIN = -(2 ** 31)
WORD_BITS = 32
SEARCH_CHUNKS = 2
RADIX_STEPS = (3,) * 10 + (2,)
VMEM_LIMIT = 48 * 1024 * 1024

COL_Q = 0
COL_K = 512
COL_V = 640
COL_IQ = 768
COL_IKW = 1280
COL_A = 1408
COL_B = 1920
W_CAT_COLS = 2432
IN_SPLIT = 1352

F32 = jnp.float32
BF16 = jnp.bfloat16
NT_DIMS = (((1,), (1,)), ((), ()))


def _cparams(sem):
    return pltpu.CompilerParams(dimension_semantics=sem, vmem_limit_bytes=VMEM_LIMIT)


def _ada_kernel(c_ref, w_ref, b_ref, o_ref):
    c = c_ref[...]
    s = (c * jax.nn.sigmoid(c)).astype(BF16)
    o_ref[0] = jnp.dot(s, w_ref[0].astype(BF16), preferred_element_type=F32) + b_ref[0]


def _ada_call(c_all, w_ada, b_ada):
    depth, d, n6 = w_ada.shape
    bp = c_all.shape[0]
    tn = 1536
    return pl.pallas_call(
        _ada_kernel,
        grid=(depth, n6 // tn),
        in_specs=[
            pl.BlockSpec((bp, d), lambda l, j: (0, 0)),
            pl.BlockSpec((1, d, tn), lambda l, j: (l, 0, j)),
            pl.BlockSpec((1, 1, tn), lambda l, j: (l, 0, j)),
        ],
        out_specs=pl.BlockSpec((1, bp, tn), lambda l, j: (l, 0, j)),
        out_shape=jax.ShapeDtypeStruct((depth, bp, n6), F32),
        compiler_params=_cparams(("arbitrary", "arbitrary")),
        name="ada_mod",
    )(c_all, w_ada, b_ada.reshape(depth, 1, n6))


def _head_mean_sq(y, bd):
    sq = y * y
    hi = sq.astype(BF16)
    lo = (sq - hi.astype(F32)).astype(BF16)
    s = jnp.dot(hi, bd, preferred_element_type=F32) + jnp.dot(lo, bd, preferred_element_type=F32)
    return s * (1.0 / HEAD_DIM)


def _rope(y, c, sa, sb):
    up = pltpu.roll(y, LANES - ROPE_DIM // 2, 1)
    dn = pltpu.roll(y, ROPE_DIM // 2, 1)
    return y * c + up * sa + dn * sb


def _inproj_kernel(x_ref, sh_ref, sc_ref, gn_ref, w_ref, gqk_ref, tabs_ref, bd_ref,
                   q_ref, iq_ref, kf_ref, vf_ref, kb_ref, vb_ref, ikwf_ref, ikwb_ref, g_ref,
                   *, bb, rt):
    m = bb * rt
    x = x_ref[...]
    ms = jnp.mean(x * x, axis=-1, keepdims=True)
    h = x * lax.rsqrt(ms + EPS) * gn_ref[...]
    h = h * (1.0 + sc_ref[...]) + sh_ref[...]
    hb = h.reshape(m, D_MODEL).astype(BF16)

    def tab(i):
        t = tabs_ref[i]
        if bb > 1:
            t = jnp.broadcast_to(t[None], (bb, rt, LANES)).reshape(m, LANES)
        return t

    cq, saq, sbq, ci, sai, sbi = (tab(i) for i in range(6))
    bd = bd_ref[...]
    gq = gqk_ref[0:1, :]
    gk = gqk_ref[1:2, :]

    def proj(c0, n):
        return jnp.dot(hb, w_ref[:, c0:c0 + n], preferred_element_type=F32)

    def put(ref, c0, val):
        ref[:, 0:rt, c0:c0 + val.shape[-1]] = val.reshape(bb, rt, val.shape[-1]).astype(ref.dtype)
        if ref.shape[1] > rt:
            ref[:, rt:, c0:c0 + val.shape[-1]] = jnp.zeros((bb, ref.shape[1] - rt, val.shape[-1]), ref.dtype)

    pq = proj(COL_Q, D_ATTN)
    for j in range(D_ATTN // LANES):
        y = pq[:, j * LANES:(j + 1) * LANES]
        y = y * lax.rsqrt(_head_mean_sq(y, bd) + EPS) * gq
        put(q_ref, j * LANES, _rope(y, cq, saq, sbq) * (HEAD_DIM ** -0.5))

    pkv = proj(COL_K, 2 * LANES)
    y = pkv[:, :LANES]
    y = y * lax.rsqrt(_head_mean_sq(y, bd) + EPS) * gk
    kk = _rope(y, cq, saq, sbq)
    put(kf_ref, 0, kk)
    put(kb_ref, 0, kk)
    vv = pkv[:, LANES:]
    put(vf_ref, 0, vv)
    put(vb_ref, 0, vv)

    piq = proj(COL_IQ, N_IDX_HEADS * IDX_DIM)
    for j in range(N_IDX_HEADS * IDX_DIM // LANES):
        put(iq_ref, j * LANES, _rope(piq[:, j * LANES:(j + 1) * LANES], cq, saq, sbq))

    ikw = _rope(proj(COL_IKW, LANES), ci, sai, sbi)
    put(ikwf_ref, 0, ikw)
    put(ikwb_ref, 0, ikw)

    pu = proj(COL_A, 2 * D_CONV)
    put(g_ref, 0, pu[:, :D_CONV] * jax.nn.sigmoid(pu[:, D_CONV:]))


def _inproj_call(x, mod, gn, w_cat, gqk, tabs, bd, *, bb, rt, tab_blocked, key_rows):
    nb, r, d = x.shape
    grid = (nb // bb, r // rt)
    assert key_rows == rt or r == rt
    tok = lambda n, dt: jax.ShapeDtypeStruct((nb, r, n), dt)
    tspec = lambda n: pl.BlockSpec((bb, rt, n), lambda i, j: (i, j, 0))
    ktok = lambda n: jax.ShapeDtypeStruct((nb, r // rt * key_rows, n), BF16)
    kspec = lambda n: pl.BlockSpec((bb, key_rows, n), lambda i, j: (i, j, 0))
    return pl.pallas_call(
        functools.partial(_inproj_kernel, bb=bb, rt=rt),
        grid=grid,
        in_specs=[
            tspec(d),
            pl.BlockSpec((bb, 1, d), lambda i, j: (i, 0, 0)),
            pl.BlockSpec((bb, 1, d), lambda i, j: (i, 0, 1)),
            pl.BlockSpec((1, d), lambda i, j: (0, 0)),
            pl.BlockSpec((d, W_CAT_COLS), lambda i, j: (0, 0)),
            pl.BlockSpec((2, LANES), lambda i, j: (0, 0)),
            pl.BlockSpec((6, rt, LANES), (lambda i, j: (0, j, 0)) if tab_blocked else (lambda i, j: (0, 0, 0))),
            pl.BlockSpec((LANES, LANES), lambda i, j: (0, 0)),
        ],
        out_specs=[tspec(D_ATTN), tspec(D_ATTN), tspec(LANES), tspec(LANES), kspec(LANES),
                   kspec(LANES), tspec(LANES), kspec(LANES), tspec(D_CONV)],
        out_shape=[tok(D_ATTN, BF16), tok(D_ATTN, BF16), tok(LANES, F32), tok(LANES, F32),
                   ktok(LANES), ktok(LANES), tok(LANES, F32), ktok(LANES),
                   tok(D_CONV, F32)],
        compiler_params=_cparams(("arbitrary", "arbitrary")),
        name="in_proj",
    )(x, mod, mod, gn, w_cat, gqk, tabs, bd)


def _stack_heads(ref, tq, heads):
    lane = lax.broadcasted_iota(jnp.int32, (tq, LANES), 1)
    low = lane < HEAD_DIM
    out = []
    for h, half in heads:
        slab = ref[0, :, (h // 2) * LANES:(h // 2 + 1) * LANES].astype(F32)
        if (h % 2) != half:
            slab = pltpu.roll(slab, HEAD_DIM, 1)
        keep = low if half == 0 else jnp.logical_not(low)
        out.append(jnp.where(keep, slab, 0.0).astype(BF16))
    return jnp.concatenate(out, axis=0)


def _tile_loop(nt, body, init):
    def run(t0, n, carry):
        for u in range(n):
            carry = body(t0 + u, carry)
        return carry

    if isinstance(nt, int):
        carry = lax.fori_loop(0, nt // 4, lambda p, c: run(4 * p, 4, c), init)
        return run(nt - nt % 4, nt % 4, carry)
    carry = lax.fori_loop(0, lax.shift_right_logical(nt, 2), lambda p, c: run(4 * p, 4, c), init)
    t2 = nt & ~3
    carry = lax.cond((nt & 2) == 2, lambda c: run(t2, 2, c), lambda c: c, carry)
    t1 = nt & ~1
    return lax.cond((nt & 1) == 1, lambda c: run(t1, 1, c), lambda c: c, carry)


def _ukey(s):
    b = lax.bitcast_convert_type(s, jnp.int32)
    return b ^ (lax.shift_right_arithmetic(b, WORD_BITS - 1) | jnp.int32(INT_MIN))


def _bit_transpose(a):
    a = list(a)
    for j, mask in ((16, 0x0000FFFF), (8, 0x00FF00FF), (4, 0x0F0F0F0F), (2, 0x33333333), (1, 0x55555555)):
        k = 0
        while k < WORD_BITS:
            t = (a[k] ^ lax.shift_right_logical(a[k + j], j)) & jnp.int32(mask)
            a[k] = a[k] ^ t
            a[k + j] = a[k + j] ^ lax.shift_left(t, j)
            k = (k + j + 1) & ~j
    return a


def _attn_kernel(*refs, tq, n_keys, causal, ktop, has_new):
    q_ref, iq_ref, ikwf_ref, k_ref, v_ref, ik_ref = refs[:6]
    new_refs = refs[6:9] if has_new else None
    tri_ref, o_ref, sc_ref, m_ref, acc_ref, kn_ref, st_ref, pa_ref = refs[6 + 3 * has_new:14 + 3 * has_new]
    pb_refs = refs[14 + 3 * has_new:]
    i = pl.program_id(1)
    sub = KEY_TILE // LANES
    n_main = k_ref.shape[1] // KEY_TILE
    if causal:
        nt = lax.shift_right_logical(i * tq + tq + KEY_TILE - 1, 8)
    else:
        nt = n_main
    nt_all = nt + 1 if has_new else nt

    def key_tile(which, t):
        if has_new and isinstance(t, int) and t == n_main:
            return new_refs[which][0]
        ref = (k_ref, v_ref, ik_ref)[which]
        return ref[0, pl.ds(pl.multiple_of(t * KEY_TILE, KEY_TILE), KEY_TILE), :]

    def all_tiles(body, init):
        carry = _tile_loop(nt, body, init)
        return body(n_main, carry) if has_new else carry

    rows = lax.broadcasted_iota(jnp.int32, (tq, LANES), 0)
    lanes = lax.broadcasted_iota(jnp.int32, (tq, LANES), 1)

    iqs = _stack_heads(iq_ref, tq, [(h, 0) for h in range(N_IDX_HEADS)])
    w8 = ikwf_ref[0][:, IDX_DIM:IDX_DIM + N_IDX_HEADS]
    wb = [jnp.broadcast_to(w8[:, h:h + 1], (tq, LANES)) for h in range(N_IDX_HEADS)]

    def score_body(t, carry):
        kt = key_tile(2, t)
        d = lax.dot_general(iqs[:, :kt.shape[1]], kt, NT_DIMS, preferred_element_type=F32)
        for c in range(sub):
            s = jnp.zeros((tq, LANES), F32)
            for h in range(N_IDX_HEADS):
                s = s + jnp.maximum(d[h * tq:(h + 1) * tq, c * LANES:(c + 1) * LANES], 0.0) * wb[h]
            kpos = t * KEY_TILE + c * LANES + lanes
            if causal:
                adm = lax.shift_right_logical(kpos, 6) <= lax.shift_right_logical(i * tq + rows, 6)
            else:
                adm = kpos < n_keys
            sc_ref[t * sub + c] = jnp.where(adm, jnp.where(s == 0.0, 0.0, s), -jnp.inf)
        return carry

    all_tiles(score_body, 0)

    def fill_body(j, carry):
        sc_ref[j] = jnp.full((tq, LANES), -jnp.inf, F32)
        return carry

    lax.fori_loop(nt_all * sub, WORD_BITS, fill_body, 0)

    def planes_body(r, carry):
        rs = pl.ds(pl.multiple_of(r * SUBLANES, SUBLANES), SUBLANES)
        words = _bit_transpose([_ukey(sc_ref[j, rs, :]) for j in range(WORD_BITS)])
        for p in range(WORD_BITS):
            pa_ref[p, rs, :] = words[p]
        for e, pb_ref in enumerate(pb_refs):
            u = _ukey(sc_ref[WORD_BITS + e, rs, :])
            for p in range(WORD_BITS):
                pb_ref[p, rs, :] = lax.shift_right_logical(u, WORD_BITS - 1 - p) & 1
        return carry

    lax.fori_loop(0, tq // SUBLANES, planes_body, 0)

    plane_refs = (pa_ref,) + tuple(pb_refs)
    ones_m = jnp.ones((LANES, LANES), BF16)

    rc = tq // SEARCH_CHUNKS

    def lane_total(parts):
        x = jnp.concatenate([p.astype(F32).astype(BF16) for p in parts], axis=0)
        tot = jnp.dot(x, ones_m, preferred_element_type=F32)
        return [tot[n * rc:(n + 1) * rc] for n in range(len(parts))]

    def popcount_sum(words):
        return sum(lax.population_count(w) for w in words)

    eqs = [[jnp.full((rc, LANES), -1, jnp.int32) for _ in plane_refs] for _ in range(SEARCH_CHUNKS)]
    cnts = [jnp.zeros((rc, LANES), F32) for _ in range(SEARCH_CHUNKS)]
    ukeys = [jnp.zeros((rc, LANES), jnp.int32) for _ in range(SEARCH_CHUNKS)]
    plane0 = 0
    for nbits in RADIX_STEPS:
        for ch in range(SEARCH_CHUNKS):
            r0 = ch * rc
            leaves = [list(eqs[ch])]
            for b in range(nbits):
                planes = [pref[plane0 + b, r0:r0 + rc, :] for pref in plane_refs]
                split = []
                for words in leaves:
                    ones = [w & p for w, p in zip(words, planes)]
                    split += [ones, [w ^ o for w, o in zip(words, ones)]]
                leaves = split
            totals = lane_total([popcount_sum(words) for words in leaves[:-1]])
            seen = [cnts[ch]]
            for tot in totals:
                seen.append(seen[-1] + tot)
            decided = [s >= ktop for s in seen[1:]]
            new_eq, new_cnt, digit = leaves[-1], seen[-1], 0
            for n in reversed(range(len(decided))):
                new_eq = [jnp.where(decided[n], a, c) for a, c in zip(leaves[n], new_eq)]
                new_cnt = jnp.where(decided[n], seen[n], new_cnt)
                digit = jnp.where(decided[n], len(decided) - n, digit)
            eqs[ch], cnts[ch] = new_eq, new_cnt
            ukeys[ch] = ukeys[ch] | lax.shift_left(digit, WORD_BITS - plane0 - nbits)
        plane0 += nbits
    ukey = jnp.concatenate(ukeys, axis=0)

    bits = jnp.where(ukey < 0, ukey ^ jnp.int32(INT_MIN), ~ukey)
    guess = lax.bitcast_convert_type(bits, F32)[:, 0:1]
    guess = jnp.where(guess == -jnp.inf, LOWEST, guess)

    def counts(thr):
        thr_b = jnp.broadcast_to(thr, (tq, LANES))

        def body(t, acc):
            for c in range(sub):
                s = sc_ref[t * sub + c]
                acc = acc + jnp.where(s > thr_b, COUNT_UNIT + 1.0, jnp.where(s >= thr_b, 1.0, 0.0))
            return acc

        acc = all_tiles(body, jnp.zeros((tq, LANES), F32))
        gt = jnp.floor(acc * (1.0 / COUNT_UNIT))
        ge = acc - COUNT_UNIT * gt
        return jnp.sum(gt, axis=1, keepdims=True), jnp.sum(ge, axis=1, keepdims=True)

    def keep(thr, n_gt, n_ge):
        for n, v in enumerate((thr, n_gt, n_ge)):
            st_ref[n] = jnp.broadcast_to(v, (tq, LANES))

    n_gt0, n_ge0 = counts(guess)
    is_kth = ((n_gt0 < ktop) & (n_ge0 >= ktop)) | ((guess == LOWEST) & (n_ge0 < ktop))
    confirmed = jnp.min(jnp.where(is_kth, 1.0, 0.0)) > 0.5
    keep(guess, n_gt0, n_ge0)

    @pl.when(jnp.logical_not(confirmed))
    def _():
        def key_to_float(key):
            return lax.bitcast_convert_type(jnp.where(key >= 0, key, key ^ jnp.int32(0x7FFFFFFF)), F32)

        def bit_body(b, key):
            cand = key + lax.shift_left(jnp.int32(1), WORD_BITS - 1 - b)
            return jnp.where(counts(key_to_float(cand))[1] >= ktop, cand, key)

        key = lax.fori_loop(0, WORD_BITS, bit_body, jnp.full((tq, 1), INT_MIN, jnp.int32))
        thr = jnp.where(key == INT_MIN, LOWEST, key_to_float(key))
        keep(thr, *counts(thr))

    thr_b = st_ref[0]
    n_gt = st_ref[1]
    has_ties = jnp.max(jnp.where(thr_b == LOWEST, 0.0, st_ref[2])) > ktop

    @pl.when(has_ties)
    def _():
        tri = tri_ref[...]
        need_b = jnp.broadcast_to(ktop - n_gt, (tq, LANES))

        def tie_body(j, seen):
            s = sc_ref[j]
            eq = s == thr_b
            pre = jnp.dot(jnp.where(eq, 1.0, 0.0).astype(BF16), tri, preferred_element_type=F32)
            sel = (s > thr_b) | (eq & (pre + seen <= need_b))
            sc_ref[j] = jnp.where(sel, SEL_MARK, -SEL_MARK)
            return seen + jnp.broadcast_to(pre[:, LANES - 1:LANES], (tq, LANES))

        lax.fori_loop(0, nt_all * sub, tie_body, jnp.zeros((tq, LANES), F32))

    sel_thr = jnp.where(has_ties, 0.0, thr_b)

    def bias(t, c):
        return jnp.where(sc_ref[t * sub + c] >= sel_thr, 0.0, NEG_BIAS)

    qs = [_stack_heads(q_ref, tq, [(g * KV_GROUP + hh, g) for hh in range(KV_GROUP)])
          for g in range(N_KV_HEADS)]
    acc_ref[...] = jnp.zeros(acc_ref.shape, F32)
    sq_rows = lax.broadcasted_iota(jnp.int32, (LANES, LANES), 0)
    sq_lanes = lax.broadcasted_iota(jnp.int32, (LANES, LANES), 1)
    head_block = (lax.shift_right_logical(sq_rows, 6) == lax.shift_right_logical(sq_lanes, 6)).astype(BF16)

    def split_dot(x, w):
        hi = x.astype(BF16)
        lo = (x - hi.astype(F32)).astype(BF16)
        return jnp.dot(hi, w, preferred_element_type=F32) + jnp.dot(lo, w, preferred_element_type=F32)

    @pl.when(i == 0)
    def _():
        def kn_body(t, mx):
            kt = key_tile(0, t).astype(F32)
            return jnp.maximum(mx, jnp.dot((kt * kt).astype(BF16), head_block, preferred_element_type=F32))

        mx = _tile_loop(n_main, kn_body, jnp.zeros((KEY_TILE, LANES), F32))
        if has_new:
            mx = kn_body(n_main, mx)
        kn_ref[...] = jnp.broadcast_to(jnp.max(mx, axis=0, keepdims=True), kn_ref.shape)

    kn = kn_ref[0:1, :]
    kn_other = pltpu.roll(kn, HEAD_DIM, 1)
    low_half = lanes[0:1, :] < HEAD_DIM
    bounds = []
    for g in range(N_KV_HEADS):
        qf = qs[g].astype(F32)
        kn_g = jnp.where(low_half, kn, kn_other) if g == 0 else jnp.where(low_half, kn_other, kn)
        bounds.append(jnp.sqrt(split_dot(qf * qf, ones_m) * kn_g) * BOUND_SLACK)
    bounded = jnp.maximum(jnp.max(bounds[0]), jnp.max(bounds[1])) < MAX_SAFE_SHIFT

    def logits(t, g):
        return lax.dot_general(qs[g], key_tile(0, t), NT_DIMS, preferred_element_type=F32)

    @pl.when(bounded)
    def _():
        for g in range(N_KV_HEADS):
            m_ref[g] = bounds[g]

    @pl.when(jnp.logical_not(bounded))
    def _():
        m_ref[...] = jnp.full(m_ref.shape, NEG_BIAS, F32)

        def max_body(t, carry):
            bs = [bias(t, c) for c in range(sub)]
            for g in range(N_KV_HEADS):
                lg = logits(t, g)
                for hh in range(KV_GROUP):
                    r0 = hh * tq
                    mm = m_ref[g, r0:r0 + tq]
                    for c in range(sub):
                        mm = jnp.maximum(mm, lg[r0:r0 + tq, c * LANES:(c + 1) * LANES] + bs[c])
                    m_ref[g, r0:r0 + tq] = mm
            return carry

        all_tiles(max_body, 0)
        for g in range(N_KV_HEADS):
            mrow = jnp.max(m_ref[g], axis=1, keepdims=True)
            m_ref[g] = jnp.broadcast_to(mrow, m_ref.shape[1:])

    def pv_body(t, carry):
        bs = [bias(t, c) for c in range(sub)]
        for g in range(N_KV_HEADS):
            lg = logits(t, g)
            ps = []
            for hh in range(KV_GROUP):
                r0 = hh * tq
                mm = m_ref[g, r0:r0 + tq]
                ps.append(jnp.concatenate(
                    [jnp.exp(lg[r0:r0 + tq, c * LANES:(c + 1) * LANES] + bs[c] - mm)
                     for c in range(sub)], axis=1).astype(BF16))
            p = jnp.concatenate(ps, axis=0)
            va = jnp.where(own_half[g], key_tile(1, t), jnp.ones((), BF16))
            acc_ref[g] += jnp.dot(p, va, preferred_element_type=F32)
        return carry

    key_lanes = lax.broadcasted_iota(jnp.int32, (KEY_TILE, LANES), 1)
    own_half = [(key_lanes < HEAD_DIM) == (g == 0) for g in range(N_KV_HEADS)]
    all_tiles(pv_body, 0)

    outs = []
    for h in range(N_HEADS):
        g, hh = divmod(h, KV_GROUP)
        a = acc_ref[g, hh * tq:(hh + 1) * tq]
        num = a[:, g * HEAD_DIM:(g + 1) * HEAD_DIM]
        den = a[:, (1 - g) * HEAD_DIM:(1 - g) * HEAD_DIM + 1]
        outs.append(num / den)
    o_ref[0] = jnp.concatenate(outs, axis=1).astype(o_ref.dtype)


def _attn_call(q, iq, ikwf, k_main, v_main, ik_main, new_tiles, tri, *, tq, n_keys, causal):
    b, t, _ = q.shape
    s_main = k_main.shape[1]
    has_new = new_tiles is not None
    s_pad = s_main + (KEY_TILE if has_new else 0)
    ktop = min(TOPK_MAX, n_keys // 4)
    n_extra = max(-(-n_keys // LANES) - WORD_BITS, 0)
    qspec = lambda n: pl.BlockSpec((1, tq, n), lambda bi, i: (bi, i, 0))
    kspec = lambda a: pl.BlockSpec((1,) + a.shape[1:], lambda bi, i: (bi, 0, 0))
    key_ops = [k_main, v_main, ik_main] + (list(new_tiles) if has_new else [])
    return pl.pallas_call(
        functools.partial(_attn_kernel, tq=tq, n_keys=n_keys, causal=causal, ktop=ktop, has_new=has_new),
        grid=(b, t // tq),
        in_specs=[qspec(D_ATTN), qspec(D_ATTN), qspec(LANES)] + [kspec(a) for a in key_ops]
                 + [pl.BlockSpec((LANES, LANES), lambda bi, i: (0, 0))],
        out_specs=qspec(D_ATTN),
        out_shape=jax.ShapeDtypeStruct((b, t, D_ATTN), BF16),
        scratch_shapes=[
            pltpu.VMEM((max(s_pad // LANES, WORD_BITS), tq, LANES), F32),
            pltpu.VMEM((N_KV_HEADS, KV_GROUP * tq, LANES), F32),
            pltpu.VMEM((N_KV_HEADS, KV_GROUP * tq, LANES), F32),
            pltpu.VMEM((SUBLANES, LANES), F32),
            pltpu.VMEM((3, tq, LANES), F32),
        ] + [pltpu.VMEM((WORD_BITS, tq, LANES), jnp.int32)] * (1 + n_extra),
        compiler_params=_cparams(("arbitrary", "arbitrary")),
        name="dsa_attention",
    )(q, iq, ikwf, *key_ops, tri)


CONV_HALO = 32
CONV_ROWS = 32


def _conv_kernel(g_ref, past_ref, w_ref, b_ref, lg_ref, lb_ref, y_ref, win_ref, *, tt):
    j = pl.program_id(1)

    @pl.when(j == 0)
    def _():
        win_ref[0, 0:CONV_HALO] = past_ref[0]

    @pl.when(j > 0)
    def _():
        win_ref[0, 0:CONV_HALO] = win_ref[0, tt:tt + CONV_HALO]

    win_ref[0, CONV_HALO:CONV_HALO + tt] = g_ref[0]
    rows = tt + CONV_HALO - SUBLANES
    for b in range(1, SUBLANES):
        win_ref[b, 0:rows] = win_ref[0, b:b + rows]
    off = CONV_HALO - (CONV_W - 1)
    bias = b_ref[...]
    lg = lg_ref[...]
    lb = lb_ref[...]
    for r0 in range(0, tt, CONV_ROWS):
        acc = jnp.zeros((CONV_ROWS // SUBLANES, SUBLANES, D_CONV), F32)
        for tap in range(CONV_W):
            a0, b = divmod(tap + off, SUBLANES)
            rows_tap = win_ref[b, r0 + a0 * SUBLANES:r0 + a0 * SUBLANES + CONV_ROWS]
            acc = acc + rows_tap.reshape(acc.shape) * w_ref[tap][None]
        acc = acc.reshape(CONV_ROWS, D_CONV)
        acc = acc + bias
        mu = jnp.mean(acc, axis=-1, keepdims=True)
        xc = acc - mu
        var = jnp.mean(xc * xc, axis=-1, keepdims=True)
        z = xc * lax.rsqrt(var + EPS) * lg + lb
        y_ref[0, r0:r0 + CONV_ROWS] = (z * jax.nn.sigmoid(z)).astype(y_ref.dtype)


def _conv_call(g, past32, conv_w, conv_b, ln_g, ln_b, *, tt):
    nb, r, _ = g.shape
    vec = pl.BlockSpec((1, D_CONV), lambda bi, j: (0, 0))
    return pl.pallas_call(
        functools.partial(_conv_kernel, tt=tt),
        grid=(nb, r // tt),
        in_specs=[
            pl.BlockSpec((1, tt, D_CONV), lambda bi, j: (bi, j, 0)),
            pl.BlockSpec((1, CONV_HALO, D_CONV), lambda bi, j: (bi, 0, 0)),
            pl.BlockSpec((CONV_W, SUBLANES, D_CONV), lambda bi, j: (0, 0, 0)),
            vec, vec, vec,
        ],
        out_specs=pl.BlockSpec((1, tt, D_CONV), lambda bi, j: (bi, j, 0)),
        out_shape=jax.ShapeDtypeStruct((nb, r, D_CONV), BF16),
        scratch_shapes=[pltpu.VMEM((SUBLANES, tt + CONV_HALO, D_CONV), F32)],
        compiler_params=_cparams(("arbitrary", "arbitrary")),
        name="conv_module",
    )(g, past32, jnp.broadcast_to(conv_w[:, None, :], (CONV_W, SUBLANES, D_CONV)), conv_b, ln_g, ln_b)


def _first_argmax(v, width):
    idx = lax.broadcasted_iota(jnp.int32, v.shape, 1)
    mx = jnp.max(v, axis=1, keepdims=True)
    first = jnp.min(jnp.where(v == mx, idx, width), axis=1, keepdims=True)
    return mx, first


def _post_kernel(x_ref, at_ref, y_ref, gta_ref, shf_ref, scf_ref, wo_ref, gn_ref, wr_ref, br_ref,
                 x1_ref, h2_ref, gate_ref, *, bb, rt):
    m = bb * rt
    at = at_ref[...].reshape(m, D_ATTN)
    yy = y_ref[...].reshape(m, D_CONV)
    mixo = (jnp.dot(at, wo_ref[0:D_ATTN], preferred_element_type=F32)
            + jnp.dot(yy, wo_ref[D_ATTN:D_ATTN + D_CONV], preferred_element_type=F32))
    x1 = x_ref[...] + gta_ref[...] * mixo.reshape(bb, rt, D_MODEL)
    x1_ref[...] = x1
    ms = jnp.mean(x1 * x1, axis=-1, keepdims=True)
    h = x1 * lax.rsqrt(ms + EPS) * gn_ref[...]
    h = h * (1.0 + scf_ref[...]) + shf_ref[...]
    hb = h.reshape(m, D_MODEL).astype(BF16)
    h2_ref[...] = hb.reshape(bb, rt, D_MODEL)

    logit = jnp.dot(hb, wr_ref[...], preferred_element_type=F32) + br_ref[...]
    gl = logit[:, 0:N_GROUPS]
    el = logit[:, N_GROUPS:N_GROUPS + N_EXPERTS]
    gmax, gsel = _first_argmax(gl, N_GROUPS)
    p_g = 1.0 / jnp.sum(jnp.exp(gl - gmax), axis=1, keepdims=True)
    ein = jnp.zeros((m, EXPERTS_PER_GROUP), F32)
    for gi in range(N_GROUPS):
        ein = ein + jnp.where(gsel == gi, el[:, gi * EXPERTS_PER_GROUP:(gi + 1) * EXPERTS_PER_GROUP], 0.0)
    e1, i1 = _first_argmax(ein, EXPERTS_PER_GROUP)
    idx4 = lax.broadcasted_iota(jnp.int32, ein.shape, 1)
    e2, i2 = _first_argmax(jnp.where(idx4 == i1, -jnp.inf, ein), EXPERTS_PER_GROUP)
    r2 = jnp.exp(e2 - e1)
    w1 = p_g / (1.0 + r2)
    w2 = p_g * r2 / (1.0 + r2)
    lane = lax.broadcasted_iota(jnp.int32, (m, LANES), 1)
    base = gsel * EXPERTS_PER_GROUP
    gate = jnp.where(lane == base + i1, w1, 0.0) + jnp.where(lane == base + i2, w2, 0.0)
    gate_ref[...] = gate.reshape(bb, rt, LANES)


def _post_call(x, attn, y, mod, w_out, gn, w_r, b_r, *, bb, rt):
    nb, r, d = x.shape
    tspec = lambda n: pl.BlockSpec((bb, rt, n), lambda i, j: (i, j, 0))
    mspec = lambda c: pl.BlockSpec((bb, 1, d), lambda i, j: (i, 0, c))
    full = lambda a, b_: pl.BlockSpec((a, b_), lambda i, j: (0, 0))
    return pl.pallas_call(
        functools.partial(_post_kernel, bb=bb, rt=rt),
        grid=(nb // bb, r // rt),
        in_specs=[tspec(d), tspec(D_ATTN), tspec(D_CONV), mspec(2), mspec(3), mspec(4),
                  full(D_ATTN + D_CONV, d), full(1, d), full(d, LANES), full(1, LANES)],
        out_specs=[tspec(d), tspec(d), tspec(LANES)],
        out_shape=[jax.ShapeDtypeStruct((nb, r, d), F32), jax.ShapeDtypeStruct((nb, r, d), BF16),
                   jax.ShapeDtypeStruct((nb, r, LANES), F32)],
        compiler_params=_cparams(("arbitrary", "arbitrary")),
        name="out_proj_router",
    )(x, attn, y, mod, mod, mod, w_out, gn, w_r, b_r)


def _moe_kernel(h_ref, gate_ref, x1_ref, gtf_ref, wg_ref, wu_ref, wd_ref, o_ref, acc_ref, *, bb, rt):
    e = pl.program_id(2)
    m = bb * rt

    @pl.when(e == 0)
    def _():
        acc_ref[...] = jnp.zeros(acc_ref.shape, F32)

    hb = h_ref[...].reshape(m, D_MODEL)
    gate = gate_ref[...].reshape(m, LANES)
    lane = lax.broadcasted_iota(jnp.int32, (m, LANES), 1)
    ge = jnp.sum(jnp.where(lane == e, gate, 0.0), axis=1, keepdims=True)
    a = jnp.dot(hb, wg_ref[0], preferred_element_type=F32)
    u = jnp.dot(hb, wu_ref[0], preferred_element_type=F32)
    hid = (a * jax.nn.sigmoid(a)) * u * ge
    acc_ref[...] += jnp.dot(hid.astype(BF16), wd_ref[0], preferred_element_type=F32)

    @pl.when(e == N_EXPERTS - 1)
    def _():
        o_ref[...] = x1_ref[...] + gtf_ref[...] * acc_ref[...].reshape(bb, rt, D_MODEL)


def _moe_call(h2, gate, x1, mod, wg, wu, wd, *, bb, rt):
    nb, r, d = x1.shape
    tspec = lambda n: pl.BlockSpec((bb, rt, n), lambda i, j, e: (i, j, 0))
    return pl.pallas_call(
        functools.partial(_moe_kernel, bb=bb, rt=rt),
        grid=(nb // bb, r // rt, N_EXPERTS),
        in_specs=[tspec(d), tspec(LANES), tspec(d),
                  pl.BlockSpec((bb, 1, d), lambda i, j, e: (i, 0, 5)),
                  pl.BlockSpec((1, d, D_EXPERT), lambda i, j, e: (e, 0, 0)),
                  pl.BlockSpec((1, d, D_EXPERT), lambda i, j, e: (e, 0, 0)),
                  pl.BlockSpec((1, D_EXPERT, d), lambda i, j, e: (e, 0, 0))],
        out_specs=tspec(d),
        out_shape=jax.ShapeDtypeStruct((nb, r, d), F32),
        scratch_shapes=[pltpu.VMEM((bb * rt, d), F32)],
        compiler_params=_cparams(("arbitrary", "arbitrary", "arbitrary")),
        name="moe",
    )(h2, gate, x1, mod, wg, wu, wd)


def _rope_tables(pos):
    half = ROPE_DIM // 2
    inv = ROPE_THETA ** (-jnp.arange(half, dtype=F32) / half)
    ang = pos.astype(F32)[:, None] * inv[None, :]
    cos, sin = jnp.cos(ang), jnp.sin(ang)
    r = pos.shape[0]
    z = lambda n: jnp.zeros((r, n), F32)
    c64 = jnp.concatenate([cos, cos, jnp.ones((r, HEAD_DIM - ROPE_DIM), F32)], axis=1)
    sa64 = jnp.concatenate([-sin, z(HEAD_DIM - half)], axis=1)
    sb64 = jnp.concatenate([z(half), sin, z(HEAD_DIM - ROPE_DIM)], axis=1)
    iw_scale = jnp.full((r, N_IDX_HEADS), N_IDX_HEADS ** -0.5 * IDX_DIM ** -0.5, F32)
    ci = jnp.concatenate([c64, iw_scale, z(LANES - IDX_DIM - N_IDX_HEADS)], axis=1)
    two = lambda a: jnp.concatenate([a, a], axis=1)
    pad = lambda a: jnp.concatenate([a, z(LANES - HEAD_DIM)], axis=1)
    return jnp.stack([two(c64), two(sa64), two(sb64), ci, pad(sa64), pad(sb64)])


def _layer_weights(l, w_in, q_norm_g, k_norm_g, conv_w, w_out, w_router_group, b_router_group,
                   w_router_expert, b_router_expert, w_gate, w_up, w_down):
    d = D_MODEL
    w_cat = jnp.concatenate(
        [w_in[l][:, :IN_SPLIT], jnp.zeros((d, COL_A - IN_SPLIT), F32), w_in[l][:, IN_SPLIT:]], axis=1).astype(BF16)
    gqk = jnp.stack([jnp.tile(q_norm_g[l], 2), jnp.tile(k_norm_g[l], 2)])
    w_r = jnp.concatenate([w_router_group[l], w_router_expert[l],
                           jnp.zeros((d, LANES - N_GROUPS - N_EXPERTS), F32)], axis=1).astype(BF16)
    b_r = jnp.concatenate([b_router_group[l], b_router_expert[l],
                           jnp.zeros((LANES - N_GROUPS - N_EXPERTS,), F32)]).reshape(1, LANES)
    return dict(w_cat=w_cat, gqk=gqk, w_out=w_out[l].astype(BF16), w_r=w_r, b_r=b_r,
                wg=w_gate[l].astype(BF16), wu=w_up[l].astype(BF16), wd=w_down[l].astype(BF16))


def _trunk_layer(x, mod, tabs, past, lw, gn_mix, gn_ffn, conv_w, conv_b, ln_g, ln_b, consts,
                 *, bb, rt, tq, tt, mbb, mrt):
    nb, r, d = x.shape
    bd, tri = consts
    q, iq, kf, vf, kb, vb, ikwf, ikwb, g = _inproj_call(
        x, mod, gn_mix.reshape(1, d), lw["w_cat"], lw["gqk"], tabs, bd, bb=bb, rt=rt, tab_blocked=past is None,
        key_rows=rt if past is None else KEY_TILE)
    if past is None:
        key_ops, new_tiles, n_keys = (kb, vb, ikwb), None, r
        past32 = jnp.zeros((nb, CONV_HALO, D_CONV), F32)
        padded_tail = g[:, r - (CONV_W - 1):]
    else:
        k_past, v_past, ik_past, conv_past = past
        past_len = k_past.shape[1]
        assert past_len % KEY_TILE == 0 and r <= KEY_TILE
        n_keys = past_len + r
        key_ops = (k_past.reshape(nb, past_len, LANES).astype(BF16),
                   v_past.reshape(nb, past_len, LANES).astype(BF16), ik_past.astype(BF16))
        new_tiles = (kb, vb, ikwb)
        past32 = jnp.concatenate([jnp.zeros((nb, CONV_HALO - (CONV_W - 1), D_CONV), F32), conv_past], axis=1)
        padded_tail = jnp.concatenate([conv_past, g], axis=1)[:, -(CONV_W - 1):]
    attn = _attn_call(q, iq, ikwf, *key_ops, new_tiles, tri, tq=tq, n_keys=n_keys, causal=past is None)
    y = _conv_call(g, past32, conv_w, conv_b.reshape(1, -1), ln_g.reshape(1, -1), ln_b.reshape(1, -1), tt=tt)
    x1, h2, gate = _post_call(x, attn, y, mod, lw["w_out"], gn_ffn.reshape(1, d), lw["w_r"], lw["b_r"], bb=bb, rt=rt)
    x2 = _moe_call(h2, gate, x1, mod, lw["wg"], lw["wu"], lw["wd"], bb=mbb, rt=mrt)
    new_k = kf.reshape(nb, r, N_KV_HEADS, HEAD_DIM)
    new_v = vf.reshape(nb, r, N_KV_HEADS, HEAD_DIM)
    return x2, (new_k, new_v, ikwf[:, :, :IDX_DIM], padded_tail)


def kernel(x_prompt, x_sample, c_prompt, c_sample, cache_k, cache_v, cache_idx_k, state_conv, w_ada, b_ada, g_norm_mix, w_in, q_norm_g, k_norm_g, conv_w, conv_b, ln_conv_g, ln_conv_b, w_out, g_norm_ffn, w_router_group, b_router_group, w_router_expert, b_router_expert, w_gate, w_up, w_down):
    depth = w_ada.shape[0]
    bp, seq, d = x_prompt.shape
    bs, dseq, _ = x_sample.shape
    past_len = cache_k.shape[2]

    n_c = bp + bs
    n_cp = -(-n_c // SUBLANES) * SUBLANES
    c_all = jnp.concatenate([c_prompt, c_sample, jnp.zeros((n_cp - n_c, d), F32)], axis=0)
    mod_all = _ada_call(c_all, w_ada, b_ada)

    blk = np.arange(LANES) // HEAD_DIM
    bd = jnp.asarray(blk[:, None] == blk[None, :], BF16)
    tri = jnp.asarray(np.arange(LANES)[:, None] <= np.arange(LANES)[None, :], BF16)
    consts = (bd, tri)

    tabs_p = _rope_tables(jnp.arange(seq, dtype=jnp.int32))
    tabs_s = _rope_tables(past_len + jnp.arange(dseq, dtype=jnp.int32))

    xp, xs = x_prompt, x_sample
    st_p, st_s = [], []
    for l in range(depth):
        lw = _layer_weights(l, w_in, q_norm_g, k_norm_g, conv_w, w_out, w_router_group, b_router_group,
                            w_router_expert, b_router_expert, w_gate, w_up, w_down)
        common = (lw, g_norm_mix[l], g_norm_ffn[l], conv_w[l], conv_b[l], ln_conv_g[l], ln_conv_b[l], consts)
        mod_p = mod_all[l, :bp].reshape(bp, 1, 6 * d)
        mod_s = mod_all[l, bp:bp + bs].reshape(bs, 1, 6 * d)
        xp, st = _trunk_layer(xp, mod_p, tabs_p, None, *common,
                              bb=1, rt=512, tq=256, tt=256, mbb=1, mrt=1024)
        st_p.append(st)
        past = (cache_k[l], cache_v[l], cache_idx_k[l], state_conv[l])
        xs, st = _trunk_layer(xs, mod_s, tabs_s, past, *common,
                              bb=8, rt=dseq, tq=dseq, tt=dseq, mbb=16, mrt=dseq)
        st_s.append(st)

    stack = lambda sts, i: jnp.stack([s[i] for s in sts])
    return (xp, xs, stack(st_p, 0), stack(st_p, 1), stack(st_p, 2), stack(st_p, 3),
            stack(st_s, 0), stack(st_s, 1), stack(st_s, 2), stack(st_s, 3))
```

```python
import functools

import jax
import jax.numpy as jnp
import numpy as np
from jax import lax
from jax.experimental import pallas as pl
from jax.experimental.pallas import tpu as pltpu

D_MODEL = 1024
CHUNK = 64
N_HEADS = 8
HEAD_DIM = 64
N_KV_HEADS = 2
KV_GROUP = N_HEADS // N_KV_HEADS
ROPE_DIM = HEAD_DIM // 4
ROPE_THETA = 500000.0
N_IDX_HEADS = 8
IDX_DIM = 64
TOPK_MAX = 256
D_ATTN = N_HEADS * HEAD_DIM
D_CONV = 512
CONV_W = 31
N_GROUPS = 4
EXPERTS_PER_GROUP = 4
N_EXPERTS = N_GROUPS * EXPERTS_PER_GROUP
D_EXPERT = 256
MOE_EXPERTS_PER_STEP = 4
ROUTER_ROWS = 32
EPS = 1e-6

LANES = 128
SUBLANES = 8
KEY_TILE = 256
NEG_BIAS = -1e30
COUNT_UNIT = 64.0
LOWEST = -3.0e38
SEL_MARK = 3.0e38
MAX_SAFE_SHIFT = 40.0
BOUND_SLACK = 1.01
INT_MIN = -(2 ** 31)
WORD_BITS = 32
SEARCH_CHUNKS = 2
RADIX_STEPS = (3,) * 10 + (2,)
VMEM_LIMIT = 48 * 1024 * 1024

COL_Q = 0
COL_K = 512
COL_V = 640
COL_IQ = 768
COL_IKW = 1280
COL_A = 1408
COL_B = 1920
W_CAT_COLS = 2432
IN_SPLIT = 1352

F32 = jnp.float32
BF16 = jnp.bfloat16
NT_DIMS = (((1,), (1,)), ((), ()))


def _cparams(sem):
    return pltpu.CompilerParams(dimension_semantics=sem, vmem_limit_bytes=VMEM_LIMIT)


def _ada_kernel(c_ref, w_ref, b_ref, o_ref):
    c = c_ref[...]
    s = (c * jax.nn.sigmoid(c)).astype(BF16)
    o_ref[0] = jnp.dot(s, w_ref[0].astype(BF16), preferred_element_type=F32) + b_ref[0]


def _ada_call(c_all, w_ada, b_ada):
    depth, d, n6 = w_ada.shape
    bp = c_all.shape[0]
    tn = 1536
    return pl.pallas_call(
        _ada_kernel,
        grid=(depth, n6 // tn),
        in_specs=[
            pl.BlockSpec((bp, d), lambda l, j: (0, 0)),
            pl.BlockSpec((1, d, tn), lambda l, j: (l, 0, j)),
            pl.BlockSpec((1, 1, tn), lambda l, j: (l, 0, j)),
        ],
        out_specs=pl.BlockSpec((1, bp, tn), lambda l, j: (l, 0, j)),
        out_shape=jax.ShapeDtypeStruct((depth, bp, n6), F32),
        compiler_params=_cparams(("arbitrary", "arbitrary")),
        name="ada_mod",
    )(c_all, w_ada, b_ada.reshape(depth, 1, n6))


def _head_mean_sq(y, bd):
    sq = y * y
    hi = sq.astype(BF16)
    lo = (sq - hi.astype(F32)).astype(BF16)
    s = jnp.dot(hi, bd, preferred_element_type=F32) + jnp.dot(lo, bd, preferred_element_type=F32)
    return s * (1.0 / HEAD_DIM)


def _rope(y, c, sa, sb):
    up = pltpu.roll(y, LANES - ROPE_DIM // 2, 1)
    dn = pltpu.roll(y, ROPE_DIM // 2, 1)
    return y * c + up * sa + dn * sb


def _inproj_kernel(x_ref, sh_ref, sc_ref, gn_ref, w_ref, gqk_ref, tabs_ref, bd_ref,
                   q_ref, iq_ref, kf_ref, vf_ref, kb_ref, vb_ref, ikwf_ref, ikwb_ref, g_ref,
                   *, bb, rt):
    m = bb * rt
    x = x_ref[...]
    ms = jnp.mean(x * x, axis=-1, keepdims=True)
    h = x * lax.rsqrt(ms + EPS) * gn_ref[...]
    h = h * (1.0 + sc_ref[...]) + sh_ref[...]
    hb = h.reshape(m, D_MODEL).astype(BF16)

    def tab(i):
        t = tabs_ref[i]
        if bb > 1:
            t = jnp.broadcast_to(t[None], (bb, rt, LANES)).reshape(m, LANES)
        return t

    cq, saq, sbq, ci, sai, sbi = (tab(i) for i in range(6))
    bd = bd_ref[...]
    gq = gqk_ref[0:1, :]
    gk = gqk_ref[1:2, :]

    def proj(c0, n):
        return jnp.dot(hb, w_ref[:, c0:c0 + n], preferred_element_type=F32)

    def put(ref, c0, val):
        ref[:, 0:rt, c0:c0 + val.shape[-1]] = val.reshape(bb, rt, val.shape[-1]).astype(ref.dtype)
        if ref.shape[1] > rt:
            ref[:, rt:, c0:c0 + val.shape[-1]] = jnp.zeros((bb, ref.shape[1] - rt, val.shape[-1]), ref.dtype)

    pq = proj(COL_Q, D_ATTN)
    for j in range(D_ATTN // LANES):
        y = pq[:, j * LANES:(j + 1) * LANES]
        y = y * lax.rsqrt(_head_mean_sq(y, bd) + EPS) * gq
        put(q_ref, j * LANES, _rope(y, cq, saq, sbq) * (HEAD_DIM ** -0.5))

    pkv = proj(COL_K, 2 * LANES)
    y = pkv[:, :LANES]
    y = y * lax.rsqrt(_head_mean_sq(y, bd) + EPS) * gk
    kk = _rope(y, cq, saq, sbq)
    put(kf_ref, 0, kk)
    put(kb_ref, 0, kk)
    vv = pkv[:, LANES:]
    put(vf_ref, 0, vv)
    put(vb_ref, 0, vv)

    piq = proj(COL_IQ, N_IDX_HEADS * IDX_DIM)
    for j in range(N_IDX_HEADS * IDX_DIM // LANES):
        put(iq_ref, j * LANES, _rope(piq[:, j * LANES:(j + 1) * LANES], cq, saq, sbq))

    ikw = _rope(proj(COL_IKW, LANES), ci, sai, sbi)
    put(ikwf_ref, 0, ikw)
    put(ikwb_ref, 0, ikw)

    pu = proj(COL_A, 2 * D_CONV)
    put(g_ref, 0, pu[:, :D_CONV] * jax.nn.sigmoid(pu[:, D_CONV:]))


def _inproj_call(x, mod, gn, w_cat, gqk, tabs, bd, *, bb, rt, tab_blocked, key_rows):
    nb, r, d = x.shape
    grid = (nb // bb, r // rt)
    assert key_rows == rt or r == rt
    tok = lambda n, dt: jax.ShapeDtypeStruct((nb, r, n), dt)
    tspec = lambda n: pl.BlockSpec((bb, rt, n), lambda i, j: (i, j, 0))
    ktok = lambda n: jax.ShapeDtypeStruct((nb, r // rt * key_rows, n), BF16)
    kspec = lambda n: pl.BlockSpec((bb, key_rows, n), lambda i, j: (i, j, 0))
    return pl.pallas_call(
        functools.partial(_inproj_kernel, bb=bb, rt=rt),
        grid=grid,
        in_specs=[
            tspec(d),
            pl.BlockSpec((bb, 1, d), lambda i, j: (i, 0, 0)),
            pl.BlockSpec((bb, 1, d), lambda i, j: (i, 0, 1)),
            pl.BlockSpec((1, d), lambda i, j: (0, 0)),
            pl.BlockSpec((d, W_CAT_COLS), lambda i, j: (0, 0)),
            pl.BlockSpec((2, LANES), lambda i, j: (0, 0)),
            pl.BlockSpec((6, rt, LANES), (lambda i, j: (0, j, 0)) if tab_blocked else (lambda i, j: (0, 0, 0))),
            pl.BlockSpec((LANES, LANES), lambda i, j: (0, 0)),
        ],
        out_specs=[tspec(D_ATTN), tspec(D_ATTN), tspec(LANES), tspec(LANES), kspec(LANES),
                   kspec(LANES), tspec(LANES), kspec(LANES), tspec(D_CONV)],
        out_shape=[tok(D_ATTN, BF16), tok(D_ATTN, BF16), tok(LANES, F32), tok(LANES, F32),
                   ktok(LANES), ktok(LANES), tok(LANES, F32), ktok(LANES),
                   tok(D_CONV, F32)],
        compiler_params=_cparams(("arbitrary", "arbitrary")),
        name="in_proj",
    )(x, mod, mod, gn, w_cat, gqk, tabs, bd)


def _stack_heads(ref, tq, heads):
    lane = lax.broadcasted_iota(jnp.int32, (tq, LANES), 1)
    low = lane < HEAD_DIM
    out = []
    for h, half in heads:
        slab = ref[0, :, (h // 2) * LANES:(h // 2 + 1) * LANES].astype(F32)
        if (h % 2) != half:
            slab = pltpu.roll(slab, HEAD_DIM, 1)
        keep = low if half == 0 else jnp.logical_not(low)
        out.append(jnp.where(keep, slab, 0.0).astype(BF16))
    return jnp.concatenate(out, axis=0)


def _tile_loop(nt, body, init):
    def run(t0, n, carry):
        for u in range(n):
            carry = body(t0 + u, carry)
        return carry

    if isinstance(nt, int):
        carry = lax.fori_loop(0, nt // 4, lambda p, c: run(4 * p, 4, c), init)
        return run(nt - nt % 4, nt % 4, carry)
    carry = lax.fori_loop(0, lax.shift_right_logical(nt, 2), lambda p, c: run(4 * p, 4, c), init)
    t2 = nt & ~3
    carry = lax.cond((nt & 2) == 2, lambda c: run(t2, 2, c), lambda c: c, carry)
    t1 = nt & ~1
    return lax.cond((nt & 1) == 1, lambda c: run(t1, 1, c), lambda c: c, carry)


def _ukey(s):
    b = lax.bitcast_convert_type(s, jnp.int32)
    return b ^ (lax.shift_right_arithmetic(b, WORD_BITS - 1) | jnp.int32(INT_MIN))


def _bit_transpose(a):
    a = list(a)
    for j, mask in ((16, 0x0000FFFF), (8, 0x00FF00FF), (4, 0x0F0F0F0F), (2, 0x33333333), (1, 0x55555555)):
        k = 0
        while k < WORD_BITS:
            t = (a[k] ^ lax.shift_right_logical(a[k + j], j)) & jnp.int32(mask)
            a[k] = a[k] ^ t
            a[k + j] = a[k + j] ^ lax.shift_left(t, j)
            k = (k + j + 1) & ~j
    return a


def _attn_kernel(*refs, tq, n_keys, causal, ktop, has_new):
    q_ref, iq_ref, ikwf_ref, k_ref, v_ref, ik_ref = refs[:6]
    new_refs = refs[6:9] if has_new else None
    tri_ref, o_ref, sc_ref, m_ref, acc_ref, kn_ref, st_ref, pa_ref = refs[6 + 3 * has_new:14 + 3 * has_new]
    pb_refs = refs[14 + 3 * has_new:]
    i = pl.program_id(1)
    sub = KEY_TILE // LANES
    n_main = k_ref.shape[1] // KEY_TILE
    if causal:
        nt = lax.shift_right_logical(i * tq + tq + KEY_TILE - 1, 8)
    else:
        nt = n_main
    nt_all = nt + 1 if has_new else nt

    def key_tile(which, t):
        if has_new and isinstance(t, int) and t == n_main:
            return new_refs[which][0]
        ref = (k_ref, v_ref, ik_ref)[which]
        return ref[0, pl.ds(pl.multiple_of(t * KEY_TILE, KEY_TILE), KEY_TILE), :]

    def all_tiles(body, init):
        carry = _tile_loop(nt, body, init)
        return body(n_main, carry) if has_new else carry

    rows = lax.broadcasted_iota(jnp.int32, (tq, LANES), 0)
    lanes = lax.broadcasted_iota(jnp.int32, (tq, LANES), 1)

    iqs = _stack_heads(iq_ref, tq, [(h, 0) for h in range(N_IDX_HEADS)])
    w8 = ikwf_ref[0][:, IDX_DIM:IDX_DIM + N_IDX_HEADS]
    wb = [jnp.broadcast_to(w8[:, h:h + 1], (tq, LANES)) for h in range(N_IDX_HEADS)]

    def score_body(t, carry):
        kt = key_tile(2, t)
        d = lax.dot_general(iqs[:, :kt.shape[1]], kt, NT_DIMS, preferred_element_type=F32)
        for c in range(sub):
            s = jnp.zeros((tq, LANES), F32)
            for h in range(N_IDX_HEADS):
                s = s + jnp.maximum(d[h * tq:(h + 1) * tq, c * LANES:(c + 1) * LANES], 0.0) * wb[h]
            kpos = t * KEY_TILE + c * LANES + lanes
            if causal:
                adm = lax.shift_right_logical(kpos, 6) <= lax.shift_right_logical(i * tq + rows, 6)
            else:
                adm = kpos < n_keys
            sc_ref[t * sub + c] = jnp.where(adm, jnp.where(s == 0.0, 0.0, s), -jnp.inf)
        return carry

    all_tiles(score_body, 0)

    def fill_body(j, carry):
        sc_ref[j] = jnp.full((tq, LANES), -jnp.inf, F32)
        return carry

    lax.fori_loop(nt_all * sub, WORD_BITS, fill_body, 0)

    def planes_body(r, carry):
        rs = pl.ds(pl.multiple_of(r * SUBLANES, SUBLANES), SUBLANES)
        words = _bit_transpose([_ukey(sc_ref[j, rs, :]) for j in range(WORD_BITS)])
        for p in range(WORD_BITS):
            pa_ref[p, rs, :] = words[p]
        for e, pb_ref in enumerate(pb_refs):
            u = _ukey(sc_ref[WORD_BITS + e, rs, :])
            for p in range(WORD_BITS):
                pb_ref[p, rs, :] = lax.shift_right_logical(u, WORD_BITS - 1 - p) & 1
        return carry

    lax.fori_loop(0, tq // SUBLANES, planes_body, 0)

    plane_refs = (pa_ref,) + tuple(pb_refs)
    ones_m = jnp.ones((LANES, LANES), BF16)

    rc = tq // SEARCH_CHUNKS

    def lane_total(parts):
        x = jnp.concatenate([p.astype(F32).astype(BF16) for p in parts], axis=0)
        tot = jnp.dot(x, ones_m, preferred_element_type=F32)
        return [tot[n * rc:(n + 1) * rc] for n in range(len(parts))]

    def popcount_sum(words):
        return sum(lax.population_count(w) for w in words)

    eqs = [[jnp.full((rc, LANES), -1, jnp.int32) for _ in plane_refs] for _ in range(SEARCH_CHUNKS)]
    cnts = [jnp.zeros((rc, LANES), F32) for _ in range(SEARCH_CHUNKS)]
    ukeys = [jnp.zeros((rc, LANES), jnp.int32) for _ in range(SEARCH_CHUNKS)]
    plane0 = 0
    for nbits in RADIX_STEPS:
        for ch in range(SEARCH_CHUNKS):
            r0 = ch * rc
            leaves = [list(eqs[ch])]
            for b in range(nbits):
                planes = [pref[plane0 + b, r0:r0 + rc, :] for pref in plane_refs]
                split = []
                for words in leaves:
                    ones = [w & p for w, p in zip(words, planes)]
                    split += [ones, [w ^ o for w, o in zip(words, ones)]]
                leaves = split
            totals = lane_total([popcount_sum(words) for words in leaves[:-1]])
            seen = [cnts[ch]]
            for tot in totals:
                seen.append(seen[-1] + tot)
            decided = [s >= ktop for s in seen[1:]]
            new_eq, new_cnt, digit = leaves[-1], seen[-1], 0
            for n in reversed(range(len(decided))):
                new_eq = [jnp.where(decided[n], a, c) for a, c in zip(leaves[n], new_eq)]
                new_cnt = jnp.where(decided[n], seen[n], new_cnt)
                digit = jnp.where(decided[n], len(decided) - n, digit)
            eqs[ch], cnts[ch] = new_eq, new_cnt
            ukeys[ch] = ukeys[ch] | lax.shift_left(digit, WORD_BITS - plane0 - nbits)
        plane0 += nbits
    ukey = jnp.concatenate(ukeys, axis=0)

    bits = jnp.where(ukey < 0, ukey ^ jnp.int32(INT_MIN), ~ukey)
    guess = lax.bitcast_convert_type(bits, F32)[:, 0:1]
    guess = jnp.where(guess == -jnp.inf, LOWEST, guess)

    def counts(thr):
        thr_b = jnp.broadcast_to(thr, (tq, LANES))

        def body(t, acc):
            for c in range(sub):
                s = sc_ref[t * sub + c]
                acc = acc + jnp.where(s > thr_b, COUNT_UNIT + 1.0, jnp.where(s >= thr_b, 1.0, 0.0))
            return acc

        acc = all_tiles(body, jnp.zeros((tq, LANES), F32))
        gt = jnp.floor(acc * (1.0 / COUNT_UNIT))
        ge = acc - COUNT_UNIT * gt
        return jnp.sum(gt, axis=1, keepdims=True), jnp.sum(ge, axis=1, keepdims=True)

    def keep(thr, n_gt, n_ge):
        for n, v in enumerate((thr, n_gt, n_ge)):
            st_ref[n] = jnp.broadcast_to(v, (tq, LANES))

    n_gt0, n_ge0 = counts(guess)
    is_kth = ((n_gt0 < ktop) & (n_ge0 >= ktop)) | ((guess == LOWEST) & (n_ge0 < ktop))
    confirmed = jnp.min(jnp.where(is_kth, 1.0, 0.0)) > 0.5
    keep(guess, n_gt0, n_ge0)

    @pl.when(jnp.logical_not(confirmed))
    def _():
        def key_to_float(key):
            return lax.bitcast_convert_type(jnp.where(key >= 0, key, key ^ jnp.int32(0x7FFFFFFF)), F32)

        def bit_body(b, key):
            cand = key + lax.shift_left(jnp.int32(1), WORD_BITS - 1 - b)
            return jnp.where(counts(key_to_float(cand))[1] >= ktop, cand, key)

        key = lax.fori_loop(0, WORD_BITS, bit_body, jnp.full((tq, 1), INT_MIN, jnp.int32))
        thr = jnp.where(key == INT_MIN, LOWEST, key_to_float(key))
        keep(thr, *counts(thr))

    thr_b = st_ref[0]
    n_gt = st_ref[1]
    has_ties = jnp.max(jnp.where(thr_b == LOWEST, 0.0, st_ref[2])) > ktop

    @pl.when(has_ties)
    def _():
        tri = tri_ref[...]
        need_b = jnp.broadcast_to(ktop - n_gt, (tq, LANES))

        def tie_body(j, seen):
            s = sc_ref[j]
            eq = s == thr_b
            pre = jnp.dot(jnp.where(eq, 1.0, 0.0).astype(BF16), tri, preferred_element_type=F32)
            sel = (s > thr_b) | (eq & (pre + seen <= need_b))
            sc_ref[j] = jnp.where(sel, SEL_MARK, -SEL_MARK)
            return seen + jnp.broadcast_to(pre[:, LANES - 1:LANES], (tq, LANES))

        lax.fori_loop(0, nt_all * sub, tie_body, jnp.zeros((tq, LANES), F32))

    sel_thr = jnp.where(has_ties, 0.0, thr_b)

    def bias(t, c):
        return jnp.where(sc_ref[t * sub + c] >= sel_thr, 0.0, NEG_BIAS)

    qs = [_stack_heads(q_ref, tq, [(g * KV_GROUP + hh, g) for hh in range(KV_GROUP)])
          for g in range(N_KV_HEADS)]
    acc_ref[...] = jnp.zeros(acc_ref.shape, F32)
    sq_rows = lax.broadcasted_iota(jnp.int32, (LANES, LANES), 0)
    sq_lanes = lax.broadcasted_iota(jnp.int32, (LANES, LANES), 1)
    head_block = (lax.shift_right_logical(sq_rows, 6) == lax.shift_right_logical(sq_lanes, 6)).astype(BF16)

    def split_dot(x, w):
        hi = x.astype(BF16)
        lo = (x - hi.astype(F32)).astype(BF16)
        return jnp.dot(hi, w, preferred_element_type=F32) + jnp.dot(lo, w, preferred_element_type=F32)

    @pl.when(i == 0)
    def _():
        def kn_body(t, mx):
            kt = key_tile(0, t).astype(F32)
            return jnp.maximum(mx, jnp.dot((kt * kt).astype(BF16), head_block, preferred_element_type=F32))

        mx = _tile_loop(n_main, kn_body, jnp.zeros((KEY_TILE, LANES), F32))
        if has_new:
            mx = kn_body(n_main, mx)
        kn_ref[...] = jnp.broadcast_to(jnp.max(mx, axis=0, keepdims=True), kn_ref.shape)

    kn = kn_ref[0:1, :]
    kn_other = pltpu.roll(kn, HEAD_DIM, 1)
    low_half = lanes[0:1, :] < HEAD_DIM
    bounds = []
    for g in range(N_KV_HEADS):
        qf = qs[g].astype(F32)
        kn_g = jnp.where(low_half, kn, kn_other) if g == 0 else jnp.where(low_half, kn_other, kn)
        bounds.append(jnp.sqrt(split_dot(qf * qf, ones_m) * kn_g) * BOUND_SLACK)
    bounded = jnp.maximum(jnp.max(bounds[0]), jnp.max(bounds[1])) < MAX_SAFE_SHIFT

    def logits(t, g):
        return lax.dot_general(qs[g], key_tile(0, t), NT_DIMS, preferred_element_type=F32)

    @pl.when(bounded)
    def _():
        for g in range(N_KV_HEADS):
            m_ref[g] = bounds[g]

    @pl.when(jnp.logical_not(bounded))
    def _():
        m_ref[...] = jnp.full(m_ref.shape, NEG_BIAS, F32)

        def max_body(t, carry):
            bs = [bias(t, c) for c in range(sub)]
            for g in range(N_KV_HEADS):
                lg = logits(t, g)
                for hh in range(KV_GROUP):
                    r0 = hh * tq
                    mm = m_ref[g, r0:r0 + tq]
                    for c in range(sub):
                        mm = jnp.maximum(mm, lg[r0:r0 + tq, c * LANES:(c + 1) * LANES] + bs[c])
                    m_ref[g, r0:r0 + tq] = mm
            return carry

        all_tiles(max_body, 0)
        for g in range(N_KV_HEADS):
            mrow = jnp.max(m_ref[g], axis=1, keepdims=True)
            m_ref[g] = jnp.broadcast_to(mrow, m_ref.shape[1:])

    def pv_body(t, carry):
        bs = [bias(t, c) for c in range(sub)]
        for g in range(N_KV_HEADS):
            lg = logits(t, g)
            ps = []
            for hh in range(KV_GROUP):
                r0 = hh * tq
                mm = m_ref[g, r0:r0 + tq]
                ps.append(jnp.concatenate(
                    [jnp.exp(lg[r0:r0 + tq, c * LANES:(c + 1) * LANES] + bs[c] - mm)
                     for c in range(sub)], axis=1).astype(BF16))
            p = jnp.concatenate(ps, axis=0)
            va = jnp.where(own_half[g], key_tile(1, t), jnp.ones((), BF16))
            acc_ref[g] += jnp.dot(p, va, preferred_element_type=F32)
        return carry

    key_lanes = lax.broadcasted_iota(jnp.int32, (KEY_TILE, LANES), 1)
    own_half = [(key_lanes < HEAD_DIM) == (g == 0) for g in range(N_KV_HEADS)]
    all_tiles(pv_body, 0)

    outs = []
    for h in range(N_HEADS):
        g, hh = divmod(h, KV_GROUP)
        a = acc_ref[g, hh * tq:(hh + 1) * tq]
        num = a[:, g * HEAD_DIM:(g + 1) * HEAD_DIM]
        den = a[:, (1 - g) * HEAD_DIM:(1 - g) * HEAD_DIM + 1]
        outs.append(num / den)
    o_ref[0] = jnp.concatenate(outs, axis=1).astype(o_ref.dtype)


def _attn_call(q, iq, ikwf, k_main, v_main, ik_main, new_tiles, tri, *, tq, n_keys, causal):
    b, t, _ = q.shape
    s_main = k_main.shape[1]
    has_new = new_tiles is not None
    s_pad = s_main + (KEY_TILE if has_new else 0)
    ktop = min(TOPK_MAX, n_keys // 4)
    n_extra = max(-(-n_keys // LANES) - WORD_BITS, 0)
    qspec = lambda n: pl.BlockSpec((1, tq, n), lambda bi, i: (bi, i, 0))
    kspec = lambda a: pl.BlockSpec((1,) + a.shape[1:], lambda bi, i: (bi, 0, 0))
    key_ops = [k_main, v_main, ik_main] + (list(new_tiles) if has_new else [])
    return pl.pallas_call(
        functools.partial(_attn_kernel, tq=tq, n_keys=n_keys, causal=causal, ktop=ktop, has_new=has_new),
        grid=(b, t // tq),
        in_specs=[qspec(D_ATTN), qspec(D_ATTN), qspec(LANES)] + [kspec(a) for a in key_ops]
                 + [pl.BlockSpec((LANES, LANES), lambda bi, i: (0, 0))],
        out_specs=qspec(D_ATTN),
        out_shape=jax.ShapeDtypeStruct((b, t, D_ATTN), BF16),
        scratch_shapes=[
            pltpu.VMEM((max(s_pad // LANES, WORD_BITS), tq, LANES), F32),
            pltpu.VMEM((N_KV_HEADS, KV_GROUP * tq, LANES), F32),
            pltpu.VMEM((N_KV_HEADS, KV_GROUP * tq, LANES), F32),
            pltpu.VMEM((SUBLANES, LANES), F32),
            pltpu.VMEM((3, tq, LANES), F32),
        ] + [pltpu.VMEM((WORD_BITS, tq, LANES), jnp.int32)] * (1 + n_extra),
        compiler_params=_cparams(("arbitrary", "arbitrary")),
        name="dsa_attention",
    )(q, iq, ikwf, *key_ops, tri)


CONV_HALO = 32
CONV_ROWS = 32


def _conv_kernel(g_ref, past_ref, w_ref, b_ref, lg_ref, lb_ref, y_ref, win_ref, *, tt):
    j = pl.program_id(1)

    @pl.when(j == 0)
    def _():
        win_ref[0, 0:CONV_HALO] = past_ref[0]

    @pl.when(j > 0)
    def _():
        win_ref[0, 0:CONV_HALO] = win_ref[0, tt:tt + CONV_HALO]

    win_ref[0, CONV_HALO:CONV_HALO + tt] = g_ref[0]
    rows = tt + CONV_HALO - SUBLANES
    for b in range(1, SUBLANES):
        win_ref[b, 0:rows] = win_ref[0, b:b + rows]
    off = CONV_HALO - (CONV_W - 1)
    bias = b_ref[...]
    lg = lg_ref[...]
    lb = lb_ref[...]
    for r0 in range(0, tt, CONV_ROWS):
        acc = jnp.zeros((CONV_ROWS // SUBLANES, SUBLANES, D_CONV), F32)
        for tap in range(CONV_W):
            a0, b = divmod(tap + off, SUBLANES)
            rows_tap = win_ref[b, r0 + a0 * SUBLANES:r0 + a0 * SUBLANES + CONV_ROWS]
            acc = acc + rows_tap.reshape(acc.shape) * w_ref[tap][None]
        acc = acc.reshape(CONV_ROWS, D_CONV)
        acc = acc + bias
        mu = jnp.mean(acc, axis=-1, keepdims=True)
        xc = acc - mu
        var = jnp.mean(xc * xc, axis=-1, keepdims=True)
        z = xc * lax.rsqrt(var + EPS) * lg + lb
        y_ref[0, r0:r0 + CONV_ROWS] = (z * jax.nn.sigmoid(z)).astype(y_ref.dtype)


def _conv_call(g, past32, conv_w, conv_b, ln_g, ln_b, *, tt):
    nb, r, _ = g.shape
    vec = pl.BlockSpec((1, D_CONV), lambda bi, j: (0, 0))
    return pl.pallas_call(
        functools.partial(_conv_kernel, tt=tt),
        grid=(nb, r // tt),
        in_specs=[
            pl.BlockSpec((1, tt, D_CONV), lambda bi, j: (bi, j, 0)),
            pl.BlockSpec((1, CONV_HALO, D_CONV), lambda bi, j: (bi, 0, 0)),
            pl.BlockSpec((CONV_W, SUBLANES, D_CONV), lambda bi, j: (0, 0, 0)),
            vec, vec, vec,
        ],
        out_specs=pl.BlockSpec((1, tt, D_CONV), lambda bi, j: (bi, j, 0)),
        out_shape=jax.ShapeDtypeStruct((nb, r, D_CONV), BF16),
        scratch_shapes=[pltpu.VMEM((SUBLANES, tt + CONV_HALO, D_CONV), F32)],
        compiler_params=_cparams(("arbitrary", "arbitrary")),
        name="conv_module",
    )(g, past32, jnp.broadcast_to(conv_w[:, None, :], (CONV_W, SUBLANES, D_CONV)), conv_b, ln_g, ln_b)


def _first_argmax(rows):
    mx = functools.reduce(jnp.maximum, rows)
    first = len(rows) - 1
    for n in reversed(range(len(rows) - 1)):
        first = jnp.where(rows[n] == mx, n, first)
    return mx, first


def _post_kernel(x_ref, at_ref, y_ref, gta_ref, shf_ref, scf_ref, wo_ref, gn_ref, wr_ref, br_ref,
                 x1_ref, h2_ref, gate_ref, gt_ref, *, bb, rt):
    m = bb * rt
    at = at_ref[...].reshape(m, D_ATTN)
    yy = y_ref[...].reshape(m, D_CONV)
    mixo = (jnp.dot(at, wo_ref[0:D_ATTN], preferred_element_type=F32)
            + jnp.dot(yy, wo_ref[D_ATTN:D_ATTN + D_CONV], preferred_element_type=F32))
    x1 = x_ref[...] + gta_ref[...] * mixo.reshape(bb, rt, D_MODEL)
    x1_ref[...] = x1
    ms = jnp.mean(x1 * x1, axis=-1, keepdims=True)
    h = x1 * lax.rsqrt(ms + EPS) * gn_ref[...]
    h = h * (1.0 + scf_ref[...]) + shf_ref[...]
    hb = h.reshape(m, D_MODEL).astype(BF16)
    h2_ref[...] = hb.reshape(bb, rt, D_MODEL)

    lt = lax.dot_general(wr_ref[...], hb, NT_DIMS, preferred_element_type=F32) + br_ref[...]
    row = lambda n: lt[n:n + 1, :]
    gmax, gsel = _first_argmax([row(g) for g in range(N_GROUPS)])
    p_g = 1.0 / sum(jnp.exp(row(g) - gmax) for g in range(N_GROUPS))
    ein = []
    for e in range(EXPERTS_PER_GROUP):
        v = row(N_GROUPS + (N_GROUPS - 1) * EXPERTS_PER_GROUP + e)
        for g in reversed(range(N_GROUPS - 1)):
            v = jnp.where(gsel == g, row(N_GROUPS + g * EXPERTS_PER_GROUP + e), v)
        ein.append(v)
    e1, i1 = _first_argmax(ein)
    e2, i2 = _first_argmax([jnp.where(i1 == e, -jnp.inf, ein[e]) for e in range(EXPERTS_PER_GROUP)])
    r2 = jnp.exp(e2 - e1)
    w1 = p_g / (1.0 + r2)
    w2 = p_g * r2 / (1.0 + r2)
    gt_ref[...] = jnp.zeros(gt_ref.shape, F32)
    for g in range(N_GROUPS):
        for e in range(EXPERTS_PER_GROUP):
            n = g * EXPERTS_PER_GROUP + e
            gt_ref[n:n + 1, :] = jnp.where(gsel == g, jnp.where(i1 == e, w1, 0.0) + jnp.where(i2 == e, w2, 0.0), 0.0)
    gate_ref[...] = gt_ref[...].T.reshape(bb, rt, LANES)


def _post_call(x, attn, y, mod, w_out, gn, w_r, b_r, *, bb, rt):
    nb, r, d = x.shape
    tspec = lambda n: pl.BlockSpec((bb, rt, n), lambda i, j: (i, j, 0))
    mspec = lambda c: pl.BlockSpec((bb, 1, d), lambda i, j: (i, 0, c))
    full = lambda a, b_: pl.BlockSpec((a, b_), lambda i, j: (0, 0))
    return pl.pallas_call(
        functools.partial(_post_kernel, bb=bb, rt=rt),
        grid=(nb // bb, r // rt),
        in_specs=[tspec(d), tspec(D_ATTN), tspec(D_CONV), mspec(2), mspec(3), mspec(4),
                  full(D_ATTN + D_CONV, d), full(1, d), full(ROUTER_ROWS, d), full(ROUTER_ROWS, 1)],
        out_specs=[tspec(d), tspec(d), tspec(LANES)],
        out_shape=[jax.ShapeDtypeStruct((nb, r, d), F32), jax.ShapeDtypeStruct((nb, r, d), BF16),
                   jax.ShapeDtypeStruct((nb, r, LANES), F32)],
        scratch_shapes=[pltpu.VMEM((LANES, bb * rt), F32)],
        compiler_params=_cparams(("arbitrary", "arbitrary")),
        name="out_proj_router",
    )(x, attn, y, mod, mod, mod, w_out, gn, w_r, b_r)


def _moe_kernel(h_ref, gate_ref, x1_ref, gtf_ref, wg_ref, wu_ref, wd_ref, o_ref, acc_ref, *, bb, rt):
    e = pl.program_id(2)
    m = bb * rt

    @pl.when(e == 0)
    def _():
        acc_ref[...] = jnp.zeros(acc_ref.shape, F32)

    hb = h_ref[...].reshape(m, D_MODEL)
    gate = gate_ref[...].reshape(m, LANES)
    lane = lax.broadcasted_iota(jnp.int32, (m, LANES), 1)
    out = jnp.zeros((m, D_MODEL), F32)
    for n in range(MOE_EXPERTS_PER_STEP):
        ge = jnp.sum(jnp.where(lane == e * MOE_EXPERTS_PER_STEP + n, gate, 0.0), axis=1, keepdims=True)
        a = jnp.dot(hb, wg_ref[n], preferred_element_type=F32)
        u = jnp.dot(hb, wu_ref[n], preferred_element_type=F32)
        hid = (a * jax.nn.sigmoid(a)) * u * ge
        out = out + jnp.dot(hid.astype(BF16), wd_ref[n], preferred_element_type=F32)
    acc_ref[...] += out

    @pl.when(e == N_EXPERTS // MOE_EXPERTS_PER_STEP - 1)
    def _():
        o_ref[...] = x1_ref[...] + gtf_ref[...] * acc_ref[...].reshape(bb, rt, D_MODEL)


def _moe_call(h2, gate, x1, mod, wg, wu, wd, *, bb, rt):
    nb, r, d = x1.shape
    tspec = lambda n: pl.BlockSpec((bb, rt, n), lambda i, j, e: (i, j, 0))
    return pl.pallas_call(
        functools.partial(_moe_kernel, bb=bb, rt=rt),
        grid=(nb // bb, r // rt, N_EXPERTS // MOE_EXPERTS_PER_STEP),
        in_specs=[tspec(d), tspec(LANES), tspec(d),
                  pl.BlockSpec((bb, 1, d), lambda i, j, e: (i, 0, 5)),
                  pl.BlockSpec((MOE_EXPERTS_PER_STEP, d, D_EXPERT), lambda i, j, e: (e, 0, 0)),
                  pl.BlockSpec((MOE_EXPERTS_PER_STEP, d, D_EXPERT), lambda i, j, e: (e, 0, 0)),
                  pl.BlockSpec((MOE_EXPERTS_PER_STEP, D_EXPERT, d), lambda i, j, e: (e, 0, 0))],
        out_specs=tspec(d),
        out_shape=jax.ShapeDtypeStruct((nb, r, d), F32),
        scratch_shapes=[pltpu.VMEM((bb * rt, d), F32)],
        compiler_params=_cparams(("arbitrary", "arbitrary", "arbitrary")),
        name="moe",
    )(h2, gate, x1, mod, wg, wu, wd)


def _rope_tables(pos):
    half = ROPE_DIM // 2
    inv = ROPE_THETA ** (-jnp.arange(half, dtype=F32) / half)
    ang = pos.astype(F32)[:, None] * inv[None, :]
    cos, sin = jnp.cos(ang), jnp.sin(ang)
    r = pos.shape[0]
    z = lambda n: jnp.zeros((r, n), F32)
    c64 = jnp.concatenate([cos, cos, jnp.ones((r, HEAD_DIM - ROPE_DIM), F32)], axis=1)
    sa64 = jnp.concatenate([-sin, z(HEAD_DIM - half)], axis=1)
    sb64 = jnp.concatenate([z(half), sin, z(HEAD_DIM - ROPE_DIM)], axis=1)
    iw_scale = jnp.full((r, N_IDX_HEADS), N_IDX_HEADS ** -0.5 * IDX_DIM ** -0.5, F32)
    ci = jnp.concatenate([c64, iw_scale, z(LANES - IDX_DIM - N_IDX_HEADS)], axis=1)
    two = lambda a: jnp.concatenate([a, a], axis=1)
    pad = lambda a: jnp.concatenate([a, z(LANES - HEAD_DIM)], axis=1)
    return jnp.stack([two(c64), two(sa64), two(sb64), ci, pad(sa64), pad(sb64)])


def _layer_weights(l, w_in, q_norm_g, k_norm_g, conv_w, w_out, w_router_group, b_router_group,
                   w_router_expert, b_router_expert, w_gate, w_up, w_down):
    d = D_MODEL
    w_cat = jnp.concatenate(
        [w_in[l][:, :IN_SPLIT], jnp.zeros((d, COL_A - IN_SPLIT), F32), w_in[l][:, IN_SPLIT:]], axis=1).astype(BF16)
    gqk = jnp.stack([jnp.tile(q_norm_g[l], 2), jnp.tile(k_norm_g[l], 2)])
    pad = ROUTER_ROWS - N_GROUPS - N_EXPERTS
    w_r = jnp.concatenate([w_router_group[l].T, w_router_expert[l].T, jnp.zeros((pad, d), F32)], axis=0).astype(BF16)
    b_r = jnp.concatenate([b_router_group[l], b_router_expert[l], jnp.zeros((pad,), F32)]).reshape(ROUTER_ROWS, 1)
    return dict(w_cat=w_cat, gqk=gqk, w_out=w_out[l].astype(BF16), w_r=w_r, b_r=b_r,
                wg=w_gate[l].astype(BF16), wu=w_up[l].astype(BF16), wd=w_down[l].astype(BF16))


def _trunk_layer(x, mod, tabs, past, lw, gn_mix, gn_ffn, conv_w, conv_b, ln_g, ln_b, consts,
                 *, bb, rt, tq, tt, mbb, mrt):
    nb, r, d = x.shape
    bd, tri = consts
    q, iq, kf, vf, kb, vb, ikwf, ikwb, g = _inproj_call(
        x, mod, gn_mix.reshape(1, d), lw["w_cat"], lw["gqk"], tabs, bd, bb=bb, rt=rt, tab_blocked=past is None,
        key_rows=rt if past is None else KEY_TILE)
    if past is None:
        key_ops, new_tiles, n_keys = (kb, vb, ikwb), None, r
        past32 = jnp.zeros((nb, CONV_HALO, D_CONV), F32)
        padded_tail = g[:, r - (CONV_W - 1):]
    else:
        k_past, v_past, ik_past, conv_past = past
        past_len = k_past.shape[1]
        assert past_len % KEY_TILE == 0 and r <= KEY_TILE
        n_keys = past_len + r
        key_ops = (k_past.astype(BF16).reshape(nb, past_len, LANES),
                   v_past.astype(BF16).reshape(nb, past_len, LANES), ik_past.astype(BF16))
        new_tiles = (kb, vb, ikwb)
        past32 = jnp.concatenate([jnp.zeros((nb, CONV_HALO - (CONV_W - 1), D_CONV), F32), conv_past], axis=1)
        padded_tail = jnp.concatenate([conv_past, g], axis=1)[:, -(CONV_W - 1):]
    attn = _attn_call(q, iq, ikwf, *key_ops, new_tiles, tri, tq=tq, n_keys=n_keys, causal=past is None)
    y = _conv_call(g, past32, conv_w, conv_b.reshape(1, -1), ln_g.reshape(1, -1), ln_b.reshape(1, -1), tt=tt)
    x1, h2, gate = _post_call(x, attn, y, mod, lw["w_out"], gn_ffn.reshape(1, d), lw["w_r"], lw["b_r"], bb=bb, rt=rt)
    x2 = _moe_call(h2, gate, x1, mod, lw["wg"], lw["wu"], lw["wd"], bb=mbb, rt=mrt)
    new_k = kf.reshape(nb, r, N_KV_HEADS, HEAD_DIM)
    new_v = vf.reshape(nb, r, N_KV_HEADS, HEAD_DIM)
    return x2, (new_k, new_v, ikwf[:, :, :IDX_DIM], padded_tail)


def kernel(x_prompt, x_sample, c_prompt, c_sample, cache_k, cache_v, cache_idx_k, state_conv, w_ada, b_ada, g_norm_mix, w_in, q_norm_g, k_norm_g, conv_w, conv_b, ln_conv_g, ln_conv_b, w_out, g_norm_ffn, w_router_group, b_router_group, w_router_expert, b_router_expert, w_gate, w_up, w_down):
    depth = w_ada.shape[0]
    bp, seq, d = x_prompt.shape
    bs, dseq, _ = x_sample.shape
    past_len = cache_k.shape[2]

    n_c = bp + bs
    n_cp = -(-n_c // SUBLANES) * SUBLANES
    c_all = jnp.concatenate([c_prompt, c_sample, jnp.zeros((n_cp - n_c, d), F32)], axis=0)
    mod_all = _ada_call(c_all, w_ada, b_ada)

    blk = np.arange(LANES) // HEAD_DIM
    bd = jnp.asarray(blk[:, None] == blk[None, :], BF16)
    tri = jnp.asarray(np.arange(LANES)[:, None] <= np.arange(LANES)[None, :], BF16)
    consts = (bd, tri)

    tabs_p = _rope_tables(jnp.arange(seq, dtype=jnp.int32))
    tabs_s = _rope_tables(past_len + jnp.arange(dseq, dtype=jnp.int32))

    xp, xs = x_prompt, x_sample
    st_p, st_s = [], []
    for l in range(depth):
        lw = _layer_weights(l, w_in, q_norm_g, k_norm_g, conv_w, w_out, w_router_group, b_router_group,
                            w_router_expert, b_router_expert, w_gate, w_up, w_down)
        common = (lw, g_norm_mix[l], g_norm_ffn[l], conv_w[l], conv_b[l], ln_conv_g[l], ln_conv_b[l], consts)
        mod_p = mod_all[l, :bp].reshape(bp, 1, 6 * d)
        mod_s = mod_all[l, bp:bp + bs].reshape(bs, 1, 6 * d)
        xp, st = _trunk_layer(xp, mod_p, tabs_p, None, *common,
                              bb=1, rt=512, tq=256, tt=256, mbb=1, mrt=1024)
        st_p.append(st)
        past = (cache_k[l], cache_v[l], cache_idx_k[l], state_conv[l])
        xs, st = _trunk_layer(xs, mod_s, tabs_s, past, *common,
                              bb=8, rt=dseq, tq=dseq, tt=dseq, mbb=16, mrt=dseq)
        st_s.append(st)

    stack = lambda sts, i: jnp.stack([s[i] for s in sts])
    return (xp, xs, stack(st_p, 0), stack(st_p, 1), stack(st_p, 2), stack(st_p, 3),
            stack(st_s, 0), stack(st_s, 1), stack(st_s, 2), stack(st_s, 3))
```

```python
import functools

import jax
import jax.numpy as jnp
import numpy as np
from jax import lax
from jax.experimental import pallas as pl
from jax.experimental.pallas import tpu as pltpu

D_MODEL = 1024
CHUNK = 64
N_HEADS = 8
HEAD_DIM = 64
N_KV_HEADS = 2
KV_GROUP = N_HEADS // N_KV_HEADS
ROPE_DIM = HEAD_DIM // 4
ROPE_THETA = 500000.0
N_IDX_HEADS = 8
IDX_DIM = 64
TOPK_MAX = 256
D_ATTN = N_HEADS * HEAD_DIM
D_CONV = 512
CONV_W = 31
N_GROUPS = 4
EXPERTS_PER_GROUP = 4
N_EXPERTS = N_GROUPS * EXPERTS_PER_GROUP
D_EXPERT = 256
MOE_EXPERTS_PER_STEP = 4
ROUTER_ROWS = 32
EPS = 1e-6

LANES = 128
SUBLANES = 8
KEY_TILE = 256
NEG_BIAS = -1e30
COUNT_UNIT = 64.0
COUNT_ROWS = 128
LOWEST = -3.0e38
SEL_MARK = 3.0e38
MAX_SAFE_SHIFT = 40.0
BOUND_SLACK = 1.01
INT_MIN = -(2 ** 31)
WORD_BITS = 32
SEARCH_CHUNKS = 4
RADIX_STEPS = (3,) * 10 + (2,)
VMEM_LIMIT = 48 * 1024 * 1024

COL_Q = 0
COL_K = 512
COL_V = 640
COL_IQ = 768
COL_IKW = 1280
COL_A = 1408
COL_B = 1920
W_CAT_COLS = 2432
IN_SPLIT = 1352

F32 = jnp.float32
BF16 = jnp.bfloat16
NT_DIMS = (((1,), (1,)), ((), ()))


def _cparams(sem):
    return pltpu.CompilerParams(dimension_semantics=sem, vmem_limit_bytes=VMEM_LIMIT)


def _ada_kernel(c_ref, w_ref, b_ref, o_ref):
    c = c_ref[...]
    s = (c * jax.nn.sigmoid(c)).astype(BF16)
    o_ref[0] = jnp.dot(s, w_ref[0].astype(BF16), preferred_element_type=F32) + b_ref[0]


def _ada_call(c_all, w_ada, b_ada):
    depth, d, n6 = w_ada.shape
    bp = c_all.shape[0]
    tn = 1536
    return pl.pallas_call(
        _ada_kernel,
        grid=(depth, n6 // tn),
        in_specs=[
            pl.BlockSpec((bp, d), lambda l, j: (0, 0)),
            pl.BlockSpec((1, d, tn), lambda l, j: (l, 0, j)),
            pl.BlockSpec((1, 1, tn), lambda l, j: (l, 0, j)),
        ],
        out_specs=pl.BlockSpec((1, bp, tn), lambda l, j: (l, 0, j)),
        out_shape=jax.ShapeDtypeStruct((depth, bp, n6), F32),
        compiler_params=_cparams(("arbitrary", "arbitrary")),
        name="ada_mod",
    )(c_all, w_ada, b_ada.reshape(depth, 1, n6))


def _head_mean_sq(y, bd):
    sq = y * y
    hi = sq.astype(BF16)
    lo = (sq - hi.astype(F32)).astype(BF16)
    s = jnp.dot(hi, bd, preferred_element_type=F32) + jnp.dot(lo, bd, preferred_element_type=F32)
    return s * (1.0 / HEAD_DIM)


def _rope(y, c, sa, sb):
    up = pltpu.roll(y, LANES - ROPE_DIM // 2, 1)
    dn = pltpu.roll(y, ROPE_DIM // 2, 1)
    return y * c + up * sa + dn * sb


def _inproj_kernel(x_ref, sh_ref, sc_ref, gn_ref, w_ref, gqk_ref, tabs_ref, bd_ref,
                   q_ref, iq_ref, kf_ref, vf_ref, kb_ref, vb_ref, ikwf_ref, ikwb_ref, g_ref,
                   *, bb, rt):
    m = bb * rt
    x = x_ref[...]
    ms = jnp.mean(x * x, axis=-1, keepdims=True)
    h = x * lax.rsqrt(ms + EPS) * gn_ref[...]
    h = h * (1.0 + sc_ref[...]) + sh_ref[...]
    hb = h.reshape(m, D_MODEL).astype(BF16)

    def tab(i):
        t = tabs_ref[i]
        if bb > 1:
            t = jnp.broadcast_to(t[None], (bb, rt, LANES)).reshape(m, LANES)
        return t

    cq, saq, sbq, ci, sai, sbi = (tab(i) for i in range(6))
    bd = bd_ref[...]
    gq = gqk_ref[0:1, :]
    gk = gqk_ref[1:2, :]

    def proj(c0, n):
        return jnp.dot(hb, w_ref[:, c0:c0 + n], preferred_element_type=F32)

    def put(ref, c0, val):
        ref[:, 0:rt, c0:c0 + val.shape[-1]] = val.reshape(bb, rt, val.shape[-1]).astype(ref.dtype)
        if ref.shape[1] > rt:
            ref[:, rt:, c0:c0 + val.shape[-1]] = jnp.zeros((bb, ref.shape[1] - rt, val.shape[-1]), ref.dtype)

    pq = proj(COL_Q, D_ATTN)
    for j in range(D_ATTN // LANES):
        y = pq[:, j * LANES:(j + 1) * LANES]
        y = y * lax.rsqrt(_head_mean_sq(y, bd) + EPS) * gq
        put(q_ref, j * LANES, _rope(y, cq, saq, sbq) * (HEAD_DIM ** -0.5))

    pkv = proj(COL_K, 2 * LANES)
    y = pkv[:, :LANES]
    y = y * lax.rsqrt(_head_mean_sq(y, bd) + EPS) * gk
    kk = _rope(y, cq, saq, sbq)
    put(kf_ref, 0, kk)
    put(kb_ref, 0, kk)
    vv = pkv[:, LANES:]
    put(vf_ref, 0, vv)
    put(vb_ref, 0, vv)

    piq = proj(COL_IQ, N_IDX_HEADS * IDX_DIM)
    for j in range(N_IDX_HEADS * IDX_DIM // LANES):
        put(iq_ref, j * LANES, _rope(piq[:, j * LANES:(j + 1) * LANES], cq, saq, sbq))

    ikw = _rope(proj(COL_IKW, LANES), ci, sai, sbi)
    put(ikwf_ref, 0, ikw)
    put(ikwb_ref, 0, ikw)

    pu = proj(COL_A, 2 * D_CONV)
    put(g_ref, 0, pu[:, :D_CONV] * jax.nn.sigmoid(pu[:, D_CONV:]))


def _inproj_call(x, mod, gn, w_cat, gqk, tabs, bd, *, bb, rt, tab_blocked, key_rows):
    nb, r, d = x.shape
    grid = (nb // bb, r // rt)
    assert key_rows == rt or r == rt
    tok = lambda n, dt: jax.ShapeDtypeStruct((nb, r, n), dt)
    tspec = lambda n: pl.BlockSpec((bb, rt, n), lambda i, j: (i, j, 0))
    ktok = lambda n: jax.ShapeDtypeStruct((nb, r // rt * key_rows, n), BF16)
    kspec = lambda n: pl.BlockSpec((bb, key_rows, n), lambda i, j: (i, j, 0))
    return pl.pallas_call(
        functools.partial(_inproj_kernel, bb=bb, rt=rt),
        grid=grid,
        in_specs=[
            tspec(d),
            pl.BlockSpec((bb, 1, d), lambda i, j: (i, 0, 0)),
            pl.BlockSpec((bb, 1, d), lambda i, j: (i, 0, 1)),
            pl.BlockSpec((1, d), lambda i, j: (0, 0)),
            pl.BlockSpec((d, W_CAT_COLS), lambda i, j: (0, 0)),
            pl.BlockSpec((2, LANES), lambda i, j: (0, 0)),
            pl.BlockSpec((6, rt, LANES), (lambda i, j: (0, j, 0)) if tab_blocked else (lambda i, j: (0, 0, 0))),
            pl.BlockSpec((LANES, LANES), lambda i, j: (0, 0)),
        ],
        out_specs=[tspec(D_ATTN), tspec(D_ATTN), tspec(LANES), tspec(LANES), kspec(LANES),
                   kspec(LANES), tspec(LANES), kspec(LANES), tspec(D_CONV)],
        out_shape=[tok(D_ATTN, BF16), tok(D_ATTN, BF16), tok(LANES, F32), tok(LANES, F32),
                   ktok(LANES), ktok(LANES), tok(LANES, F32), ktok(LANES),
                   tok(D_CONV, F32)],
        compiler_params=_cparams(("arbitrary", "arbitrary")),
        name="in_proj",
    )(x, mod, mod, gn, w_cat, gqk, tabs, bd)


def _stack_heads(ref, tq, heads):
    lane = lax.broadcasted_iota(jnp.int32, (tq, LANES), 1)
    low = lane < HEAD_DIM
    out = []
    for h, half in heads:
        slab = ref[0, :, (h // 2) * LANES:(h // 2 + 1) * LANES].astype(F32)
        if (h % 2) != half:
            slab = pltpu.roll(slab, HEAD_DIM, 1)
        keep = low if half == 0 else jnp.logical_not(low)
        out.append(jnp.where(keep, slab, 0.0).astype(BF16))
    return jnp.concatenate(out, axis=0)


def _tile_loop(nt, body, init):
    def run(t0, n, carry):
        for u in range(n):
            carry = body(t0 + u, carry)
        return carry

    if isinstance(nt, int):
        carry = lax.fori_loop(0, nt // 4, lambda p, c: run(4 * p, 4, c), init)
        return run(nt - nt % 4, nt % 4, carry)
    carry = lax.fori_loop(0, lax.shift_right_logical(nt, 2), lambda p, c: run(4 * p, 4, c), init)
    t2 = nt & ~3
    carry = lax.cond((nt & 2) == 2, lambda c: run(t2, 2, c), lambda c: c, carry)
    t1 = nt & ~1
    return lax.cond((nt & 1) == 1, lambda c: run(t1, 1, c), lambda c: c, carry)


def _ukey(s):
    b = lax.bitcast_convert_type(s, jnp.int32)
    return b ^ (lax.shift_right_arithmetic(b, WORD_BITS - 1) | jnp.int32(INT_MIN))


def _bit_transpose(a):
    a = list(a)
    for j, mask in ((16, 0x0000FFFF), (8, 0x00FF00FF), (4, 0x0F0F0F0F), (2, 0x33333333), (1, 0x55555555)):
        k = 0
        while k < WORD_BITS:
            t = (a[k] ^ lax.shift_right_logical(a[k + j], j)) & jnp.int32(mask)
            a[k] = a[k] ^ t
            a[k + j] = a[k + j] ^ lax.shift_left(t, j)
            k = (k + j + 1) & ~j
    return a


def _attn_kernel(*refs, tq, n_keys, causal, ktop, has_new):
    q_ref, iq_ref, ikwf_ref, k_ref, v_ref, ik_ref = refs[:6]
    new_refs = refs[6:9] if has_new else None
    tri_ref, o_ref, sc_ref, m_ref, acc_ref, kn_ref, st_ref, pa_ref = refs[6 + 3 * has_new:14 + 3 * has_new]
    pb_refs = refs[14 + 3 * has_new:]
    i = pl.program_id(1)
    sub = KEY_TILE // LANES
    n_main = k_ref.shape[1] // KEY_TILE
    if causal:
        nt = lax.shift_right_logical(i * tq + tq + KEY_TILE - 1, 8)
    else:
        nt = n_main
    nt_all = nt + 1 if has_new else nt

    def key_tile(which, t):
        if has_new and isinstance(t, int) and t == n_main:
            return new_refs[which][0]
        ref = (k_ref, v_ref, ik_ref)[which]
        return ref[0, pl.ds(pl.multiple_of(t * KEY_TILE, KEY_TILE), KEY_TILE), :]

    def all_tiles(body, init):
        carry = _tile_loop(nt, body, init)
        return body(n_main, carry) if has_new else carry

    rows = lax.broadcasted_iota(jnp.int32, (tq, LANES), 0)
    lanes = lax.broadcasted_iota(jnp.int32, (tq, LANES), 1)

    iqs = _stack_heads(iq_ref, tq, [(h, 0) for h in range(N_IDX_HEADS)])
    w8 = ikwf_ref[0][:, IDX_DIM:IDX_DIM + N_IDX_HEADS]
    wb = [jnp.broadcast_to(w8[:, h:h + 1], (tq, LANES)) for h in range(N_IDX_HEADS)]

    def score_body(t, carry):
        kt = key_tile(2, t)
        d = lax.dot_general(iqs[:, :kt.shape[1]], kt, NT_DIMS, preferred_element_type=F32)
        for c in range(sub):
            s = jnp.zeros((tq, LANES), F32)
            for h in range(N_IDX_HEADS):
                s = s + jnp.maximum(d[h * tq:(h + 1) * tq, c * LANES:(c + 1) * LANES], 0.0) * wb[h]
            kpos = t * KEY_TILE + c * LANES + lanes
            if causal:
                adm = lax.shift_right_logical(kpos, 6) <= lax.shift_right_logical(i * tq + rows, 6)
            else:
                adm = kpos < n_keys
            sc_ref[t * sub + c] = jnp.where(adm, jnp.where(s == 0.0, 0.0, s), -jnp.inf)
        return carry

    all_tiles(score_body, 0)

    def fill_body(j, carry):
        sc_ref[j] = jnp.full((tq, LANES), -jnp.inf, F32)
        return carry

    lax.fori_loop(nt_all * sub, WORD_BITS, fill_body, 0)

    def planes_body(r, carry):
        rs = pl.ds(pl.multiple_of(r * SUBLANES, SUBLANES), SUBLANES)
        words = _bit_transpose([_ukey(sc_ref[j, rs, :]) for j in range(WORD_BITS)])
        for p in range(WORD_BITS):
            pa_ref[p, rs, :] = words[p]
        for e, pb_ref in enumerate(pb_refs):
            u = _ukey(sc_ref[WORD_BITS + e, rs, :])
            for p in range(WORD_BITS):
                pb_ref[p, rs, :] = lax.shift_right_logical(u, WORD_BITS - 1 - p) & 1
        return carry

    lax.fori_loop(0, tq // SUBLANES, planes_body, 0)

    plane_refs = (pa_ref,) + tuple(pb_refs)
    ones_m = jnp.ones((LANES, LANES), BF16)

    rc = tq // SEARCH_CHUNKS

    def lane_total(parts):
        x = jnp.concatenate([p.astype(F32).astype(BF16) for p in parts], axis=0)
        tot = jnp.dot(x, ones_m, preferred_element_type=F32)
        return [tot[n * rc:(n + 1) * rc] for n in range(len(parts))]

    def popcount_sum(words):
        return sum(lax.population_count(w) for w in words)

    eqs = [[jnp.full((rc, LANES), -1, jnp.int32) for _ in plane_refs] for _ in range(SEARCH_CHUNKS)]
    cnts = [jnp.zeros((rc, LANES), F32) for _ in range(SEARCH_CHUNKS)]
    ukeys = [jnp.zeros((rc, LANES), jnp.int32) for _ in range(SEARCH_CHUNKS)]
    plane0 = 0
    for nbits in RADIX_STEPS:
        for ch in range(SEARCH_CHUNKS):
            r0 = ch * rc
            leaves = [list(eqs[ch])]
            for b in range(nbits):
                planes = [pref[plane0 + b, r0:r0 + rc, :] for pref in plane_refs]
                split = []
                for words in leaves:
                    ones = [w & p for w, p in zip(words, planes)]
                    split += [ones, [w ^ o for w, o in zip(words, ones)]]
                leaves = split
            totals = lane_total([popcount_sum(words) for words in leaves[:-1]])
            seen = [cnts[ch]]
            for tot in totals:
                seen.append(seen[-1] + tot)
            decided = [s >= ktop for s in seen[1:]]
            new_eq, new_cnt, digit = leaves[-1], seen[-1], 0
            for n in reversed(range(len(decided))):
                new_eq = [jnp.where(decided[n], a, c) for a, c in zip(leaves[n], new_eq)]
                new_cnt = jnp.where(decided[n], seen[n], new_cnt)
                digit = jnp.where(decided[n], len(decided) - n, digit)
            eqs[ch], cnts[ch] = new_eq, new_cnt
            ukeys[ch] = ukeys[ch] | lax.shift_left(digit, WORD_BITS - plane0 - nbits)
        plane0 += nbits
    ukey = jnp.concatenate(ukeys, axis=0)

    bits = jnp.where(ukey < 0, ukey ^ jnp.int32(INT_MIN), ~ukey)
    guess = lax.bitcast_convert_type(bits, F32)[:, 0:1]
    guess = jnp.where(guess == -jnp.inf, LOWEST, guess)

    def counts(thr):
        rb = min(tq, COUNT_ROWS)
        accs = []
        for r0 in range(0, tq, rb):
            thr_b = jnp.broadcast_to(thr[r0:r0 + rb], (rb, LANES))

            def body(t, acc, r0=r0, thr_b=thr_b):
                for c in range(sub):
                    s = sc_ref[t * sub + c, r0:r0 + rb, :]
                    acc = acc + jnp.where(s > thr_b, COUNT_UNIT + 1.0, jnp.where(s >= thr_b, 1.0, 0.0))
                return acc

            accs.append(all_tiles(body, jnp.zeros((rb, LANES), F32)))
        acc = jnp.concatenate(accs, axis=0)
        gt = jnp.floor(acc * (1.0 / COUNT_UNIT))
        ge = acc - COUNT_UNIT * gt
        return jnp.sum(gt, axis=1, keepdims=True), jnp.sum(ge, axis=1, keepdims=True)

    def keep(thr, n_gt, n_ge):
        for n, v in enumerate((thr, n_gt, n_ge)):
            st_ref[n] = jnp.broadcast_to(v, (tq, LANES))

    n_gt0, n_ge0 = counts(guess)
    is_kth = ((n_gt0 < ktop) & (n_ge0 >= ktop)) | ((guess == LOWEST) & (n_ge0 < ktop))
    confirmed = jnp.min(jnp.where(is_kth, 1.0, 0.0)) > 0.5
    keep(guess, n_gt0, n_ge0)

    @pl.when(jnp.logical_not(confirmed))
    def _():
        def key_to_float(key):
            return lax.bitcast_convert_type(jnp.where(key >= 0, key, key ^ jnp.int32(0x7FFFFFFF)), F32)

        def bit_body(b, key):
            cand = key + lax.shift_left(jnp.int32(1), WORD_BITS - 1 - b)
            return jnp.where(counts(key_to_float(cand))[1] >= ktop, cand, key)

        key = lax.fori_loop(0, WORD_BITS, bit_body, jnp.full((tq, 1), INT_MIN, jnp.int32))
        thr = jnp.where(key == INT_MIN, LOWEST, key_to_float(key))
        keep(thr, *counts(thr))

    thr_b = st_ref[0]
    n_gt = st_ref[1]
    has_ties = jnp.max(jnp.where(thr_b == LOWEST, 0.0, st_ref[2])) > ktop

    @pl.when(has_ties)
    def _():
        tri = tri_ref[...]
        need_b = jnp.broadcast_to(ktop - n_gt, (tq, LANES))

        def tie_body(j, seen):
            s = sc_ref[j]
            eq = s == thr_b
            pre = jnp.dot(jnp.where(eq, 1.0, 0.0).astype(BF16), tri, preferred_element_type=F32)
            sel = (s > thr_b) | (eq & (pre + seen <= need_b))
            sc_ref[j] = jnp.where(sel, SEL_MARK, -SEL_MARK)
            return seen + jnp.broadcast_to(pre[:, LANES - 1:LANES], (tq, LANES))

        lax.fori_loop(0, nt_all * sub, tie_body, jnp.zeros((tq, LANES), F32))

    sel_thr = jnp.where(has_ties, 0.0, thr_b)

    def bias(t, c):
        return jnp.where(sc_ref[t * sub + c] >= sel_thr, 0.0, NEG_BIAS)

    qs = [_stack_heads(q_ref, tq, [(g * KV_GROUP + hh, g) for hh in range(KV_GROUP)])
          for g in range(N_KV_HEADS)]
    acc_ref[...] = jnp.zeros(acc_ref.shape, F32)
    sq_rows = lax.broadcasted_iota(jnp.int32, (LANES, LANES), 0)
    sq_lanes = lax.broadcasted_iota(jnp.int32, (LANES, LANES), 1)
    head_block = (lax.shift_right_logical(sq_rows, 6) == lax.shift_right_logical(sq_lanes, 6)).astype(BF16)

    @pl.when(i == 0)
    def _():
        def kn_body(t, mx):
            kt = key_tile(0, t).astype(F32)
            return jnp.maximum(mx, jnp.dot((kt * kt).astype(BF16), head_block, preferred_element_type=F32))

        mx = _tile_loop(n_main, kn_body, jnp.zeros((KEY_TILE, LANES), F32))
        if has_new:
            mx = kn_body(n_main, mx)
        kn_ref[...] = jnp.broadcast_to(jnp.max(mx, axis=0, keepdims=True), kn_ref.shape)

    kn = kn_ref[0:1, :]
    kn_other = pltpu.roll(kn, HEAD_DIM, 1)
    low_half = lanes[0:1, :] < HEAD_DIM
    bounds = []
    for g in range(N_KV_HEADS):
        qf = qs[g].astype(F32)
        kn_g = jnp.where(low_half, kn, kn_other) if g == 0 else jnp.where(low_half, kn_other, kn)
        qn = jnp.dot((qf * qf).astype(BF16), ones_m, preferred_element_type=F32)
        bounds.append(jnp.sqrt(qn * kn_g) * BOUND_SLACK)
    bounded = jnp.maximum(jnp.max(bounds[0]), jnp.max(bounds[1])) < MAX_SAFE_SHIFT

    def logits(t, g):
        return lax.dot_general(qs[g], key_tile(0, t), NT_DIMS, preferred_element_type=F32)

    @pl.when(bounded)
    def _():
        for g in range(N_KV_HEADS):
            m_ref[g] = bounds[g]

    @pl.when(jnp.logical_not(bounded))
    def _():
        m_ref[...] = jnp.full(m_ref.shape, NEG_BIAS, F32)

        def max_body(t, carry):
            bs = [bias(t, c) for c in range(sub)]
            for g in range(N_KV_HEADS):
                lg = logits(t, g)
                for hh in range(KV_GROUP):
                    r0 = hh * tq
                    mm = m_ref[g, r0:r0 + tq]
                    for c in range(sub):
                        mm = jnp.maximum(mm, lg[r0:r0 + tq, c * LANES:(c + 1) * LANES] + bs[c])
                    m_ref[g, r0:r0 + tq] = mm
            return carry

        all_tiles(max_body, 0)
        for g in range(N_KV_HEADS):
            mrow = jnp.max(m_ref[g], axis=1, keepdims=True)
            m_ref[g] = jnp.broadcast_to(mrow, m_ref.shape[1:])

    def pv_body(t, carry):
        bs = [bias(t, c) for c in range(sub)]
        for g in range(N_KV_HEADS):
            lg = logits(t, g)
            ps = []
            for hh in range(KV_GROUP):
                r0 = hh * tq
                mm = m_ref[g, r0:r0 + tq]
                ps.append(jnp.concatenate(
                    [jnp.exp(lg[r0:r0 + tq, c * LANES:(c + 1) * LANES] + bs[c] - mm)
                     for c in range(sub)], axis=1).astype(BF16))
            p = jnp.concatenate(ps, axis=0)
            va = jnp.where(own_half[g], key_tile(1, t), jnp.ones((), BF16))
            acc_ref[g] += jnp.dot(p, va, preferred_element_type=F32)
        return carry

    key_lanes = lax.broadcasted_iota(jnp.int32, (KEY_TILE, LANES), 1)
    own_half = [(key_lanes < HEAD_DIM) == (g == 0) for g in range(N_KV_HEADS)]
    all_tiles(pv_body, 0)

    outs = []
    for h in range(N_HEADS):
        g, hh = divmod(h, KV_GROUP)
        a = acc_ref[g, hh * tq:(hh + 1) * tq]
        num = a[:, g * HEAD_DIM:(g + 1) * HEAD_DIM]
        den = a[:, (1 - g) * HEAD_DIM:(1 - g) * HEAD_DIM + 1]
        outs.append(num / den)
    o_ref[0] = jnp.concatenate(outs, axis=1).astype(o_ref.dtype)


def _attn_call(q, iq, ikwf, k_main, v_main, ik_main, new_tiles, tri, *, tq, n_keys, causal):
    b, t, _ = q.shape
    s_main = k_main.shape[1]
    has_new = new_tiles is not None
    s_pad = s_main + (KEY_TILE if has_new else 0)
    ktop = min(TOPK_MAX, n_keys // 4)
    n_extra = max(-(-n_keys // LANES) - WORD_BITS, 0)
    qspec = lambda n: pl.BlockSpec((1, tq, n), lambda bi, i: (bi, i, 0))
    kspec = lambda a: pl.BlockSpec((1,) + a.shape[1:], lambda bi, i: (bi, 0, 0))
    key_ops = [k_main, v_main, ik_main] + (list(new_tiles) if has_new else [])
    return pl.pallas_call(
        functools.partial(_attn_kernel, tq=tq, n_keys=n_keys, causal=causal, ktop=ktop, has_new=has_new),
        grid=(b, t // tq),
        in_specs=[qspec(D_ATTN), qspec(D_ATTN), qspec(LANES)] + [kspec(a) for a in key_ops]
                 + [pl.BlockSpec((LANES, LANES), lambda bi, i: (0, 0))],
        out_specs=qspec(D_ATTN),
        out_shape=jax.ShapeDtypeStruct((b, t, D_ATTN), BF16),
        scratch_shapes=[
            pltpu.VMEM((max(s_pad // LANES, WORD_BITS), tq, LANES), F32),
            pltpu.VMEM((N_KV_HEADS, KV_GROUP * tq, LANES), F32),
            pltpu.VMEM((N_KV_HEADS, KV_GROUP * tq, LANES), F32),
            pltpu.VMEM((SUBLANES, LANES), F32),
            pltpu.VMEM((3, tq, LANES), F32),
        ] + [pltpu.VMEM((WORD_BITS, tq, LANES), jnp.int32)] * (1 + n_extra),
        compiler_params=_cparams(("arbitrary", "arbitrary")),
        name="dsa_attention",
    )(q, iq, ikwf, *key_ops, tri)


CONV_HALO = 32
CONV_ROWS = 32


def _conv_kernel(g_ref, past_ref, w_ref, b_ref, lg_ref, lb_ref, y_ref, win_ref, *, tt):
    j = pl.program_id(1)

    @pl.when(j == 0)
    def _():
        win_ref[0, 0:CONV_HALO] = past_ref[0]

    @pl.when(j > 0)
    def _():
        win_ref[0, 0:CONV_HALO] = win_ref[0, tt:tt + CONV_HALO]

    win_ref[0, CONV_HALO:CONV_HALO + tt] = g_ref[0]
    rows = tt + CONV_HALO - SUBLANES
    for b in range(1, SUBLANES):
        win_ref[b, 0:rows] = win_ref[0, b:b + rows]
    off = CONV_HALO - (CONV_W - 1)
    bias = b_ref[...]
    lg = lg_ref[...]
    lb = lb_ref[...]
    for r0 in range(0, tt, CONV_ROWS):
        acc = jnp.zeros((CONV_ROWS // SUBLANES, SUBLANES, D_CONV), F32)
        for tap in range(CONV_W):
            a0, b = divmod(tap + off, SUBLANES)
            rows_tap = win_ref[b, r0 + a0 * SUBLANES:r0 + a0 * SUBLANES + CONV_ROWS]
            acc = acc + rows_tap.reshape(acc.shape) * w_ref[tap][None]
        acc = acc.reshape(CONV_ROWS, D_CONV)
        acc = acc + bias
        mu = jnp.mean(acc, axis=-1, keepdims=True)
        xc = acc - mu
        var = jnp.mean(xc * xc, axis=-1, keepdims=True)
        z = xc * lax.rsqrt(var + EPS) * lg + lb
        y_ref[0, r0:r0 + CONV_ROWS] = (z * jax.nn.sigmoid(z)).astype(y_ref.dtype)


def _conv_call(g, past32, conv_w, conv_b, ln_g, ln_b, *, tt):
    nb, r, _ = g.shape
    vec = pl.BlockSpec((1, D_CONV), lambda bi, j: (0, 0))
    return pl.pallas_call(
        functools.partial(_conv_kernel, tt=tt),
        grid=(nb, r // tt),
        in_specs=[
            pl.BlockSpec((1, tt, D_CONV), lambda bi, j: (bi, j, 0)),
            pl.BlockSpec((1, CONV_HALO, D_CONV), lambda bi, j: (bi, 0, 0)),
            pl.BlockSpec((CONV_W, SUBLANES, D_CONV), lambda bi, j: (0, 0, 0)),
            vec, vec, vec,
        ],
        out_specs=pl.BlockSpec((1, tt, D_CONV), lambda bi, j: (bi, j, 0)),
        out_shape=jax.ShapeDtypeStruct((nb, r, D_CONV), BF16),
        scratch_shapes=[pltpu.VMEM((SUBLANES, tt + CONV_HALO, D_CONV), F32)],
        compiler_params=_cparams(("arbitrary", "arbitrary")),
        name="conv_module",
    )(g, past32, jnp.broadcast_to(conv_w[:, None, :], (CONV_W, SUBLANES, D_CONV)), conv_b, ln_g, ln_b)


def _first_argmax(rows):
    mx = functools.reduce(jnp.maximum, rows)
    first = len(rows) - 1
    for n in reversed(range(len(rows) - 1)):
        first = jnp.where(rows[n] == mx, n, first)
    return mx, first


def _post_kernel(x_ref, at_ref, y_ref, gta_ref, shf_ref, scf_ref, wo_ref, gn_ref, wr_ref, br_ref,
                 x1_ref, h2_ref, gate_ref, gt_ref, *, bb, rt):
    m = bb * rt
    at = at_ref[...].reshape(m, D_ATTN)
    yy = y_ref[...].reshape(m, D_CONV)
    mixo = (jnp.dot(at, wo_ref[0:D_ATTN], preferred_element_type=F32)
            + jnp.dot(yy, wo_ref[D_ATTN:D_ATTN + D_CONV], preferred_element_type=F32))
    x1 = x_ref[...] + gta_ref[...] * mixo.reshape(bb, rt, D_MODEL)
    x1_ref[...] = x1
    ms = jnp.mean(x1 * x1, axis=-1, keepdims=True)
    h = x1 * lax.rsqrt(ms + EPS) * gn_ref[...]
    h = h * (1.0 + scf_ref[...]) + shf_ref[...]
    hb = h.reshape(m, D_MODEL).astype(BF16)
    h2_ref[...] = hb.reshape(bb, rt, D_MODEL)

    lt = lax.dot_general(wr_ref[...], hb, NT_DIMS, preferred_element_type=F32) + br_ref[...]
    row = lambda n: lt[n:n + 1, :]
    gmax, gsel = _first_argmax([row(g) for g in range(N_GROUPS)])
    p_g = 1.0 / sum(jnp.exp(row(g) - gmax) for g in range(N_GROUPS))
    ein = []
    for e in range(EXPERTS_PER_GROUP):
        v = row(N_GROUPS + (N_GROUPS - 1) * EXPERTS_PER_GROUP + e)
        for g in reversed(range(N_GROUPS - 1)):
            v = jnp.where(gsel == g, row(N_GROUPS + g * EXPERTS_PER_GROUP + e), v)
        ein.append(v)
    e1, i1 = _first_argmax(ein)
    e2, i2 = _first_argmax([jnp.where(i1 == e, -jnp.inf, ein[e]) for e in range(EXPERTS_PER_GROUP)])
    r2 = jnp.exp(e2 - e1)
    w1 = p_g / (1.0 + r2)
    w2 = p_g * r2 / (1.0 + r2)
    gt_ref[...] = jnp.zeros(gt_ref.shape, F32)
    for g in range(N_GROUPS):
        for e in range(EXPERTS_PER_GROUP):
            n = g * EXPERTS_PER_GROUP + e
            gt_ref[n:n + 1, :] = jnp.where(gsel == g, jnp.where(i1 == e, w1, 0.0) + jnp.where(i2 == e, w2, 0.0), 0.0)
    gate_ref[...] = gt_ref[...].T.reshape(bb, rt, LANES)


def _post_call(x, attn, y, mod, w_out, gn, w_r, b_r, *, bb, rt):
    nb, r, d = x.shape
    tspec = lambda n: pl.BlockSpec((bb, rt, n), lambda i, j: (i, j, 0))
    mspec = lambda c: pl.BlockSpec((bb, 1, d), lambda i, j: (i, 0, c))
    full = lambda a, b_: pl.BlockSpec((a, b_), lambda i, j: (0, 0))
    return pl.pallas_call(
        functools.partial(_post_kernel, bb=bb, rt=rt),
        grid=(nb // bb, r // rt),
        in_specs=[tspec(d), tspec(D_ATTN), tspec(D_CONV), mspec(2), mspec(3), mspec(4),
                  full(D_ATTN + D_CONV, d), full(1, d), full(ROUTER_ROWS, d), full(ROUTER_ROWS, 1)],
        out_specs=[tspec(d), tspec(d), tspec(LANES)],
        out_shape=[jax.ShapeDtypeStruct((nb, r, d), F32), jax.ShapeDtypeStruct((nb, r, d), BF16),
                   jax.ShapeDtypeStruct((nb, r, LANES), F32)],
        scratch_shapes=[pltpu.VMEM((LANES, bb * rt), F32)],
        compiler_params=_cparams(("arbitrary", "arbitrary")),
        name="out_proj_router",
    )(x, attn, y, mod, mod, mod, w_out, gn, w_r, b_r)


def _moe_kernel(h_ref, gate_ref, x1_ref, gtf_ref, wg_ref, wu_ref, wd_ref, o_ref, acc_ref, *, bb, rt):
    e = pl.program_id(2)
    m = bb * rt

    @pl.when(e == 0)
    def _():
        acc_ref[...] = jnp.zeros(acc_ref.shape, F32)

    hb = h_ref[...].reshape(m, D_MODEL)
    gate = gate_ref[...].reshape(m, LANES)
    lane = lax.broadcasted_iota(jnp.int32, (m, LANES), 1)
    out = jnp.zeros((m, D_MODEL), F32)
    for n in range(MOE_EXPERTS_PER_STEP):
        ge = jnp.sum(jnp.where(lane == e * MOE_EXPERTS_PER_STEP + n, gate, 0.0), axis=1, keepdims=True)
        a = jnp.dot(hb, wg_ref[n], preferred_element_type=F32)
        u = jnp.dot(hb, wu_ref[n], preferred_element_type=F32)
        hid = (a * jax.nn.sigmoid(a)) * u * ge
        out = out + jnp.dot(hid.astype(BF16), wd_ref[n], preferred_element_type=F32)
    acc_ref[...] += out

    @pl.when(e == N_EXPERTS // MOE_EXPERTS_PER_STEP - 1)
    def _():
        o_ref[...] = x1_ref[...] + gtf_ref[...] * acc_ref[...].reshape(bb, rt, D_MODEL)


def _moe_call(h2, gate, x1, mod, wg, wu, wd, *, bb, rt):
    nb, r, d = x1.shape
    tspec = lambda n: pl.BlockSpec((bb, rt, n), lambda i, j, e: (i, j, 0))
    return pl.pallas_call(
        functools.partial(_moe_kernel, bb=bb, rt=rt),
        grid=(nb // bb, r // rt, N_EXPERTS // MOE_EXPERTS_PER_STEP),
        in_specs=[tspec(d), tspec(LANES), tspec(d),
                  pl.BlockSpec((bb, 1, d), lambda i, j, e: (i, 0, 5)),
                  pl.BlockSpec((MOE_EXPERTS_PER_STEP, d, D_EXPERT), lambda i, j, e: (e, 0, 0)),
                  pl.BlockSpec((MOE_EXPERTS_PER_STEP, d, D_EXPERT), lambda i, j, e: (e, 0, 0)),
                  pl.BlockSpec((MOE_EXPERTS_PER_STEP, D_EXPERT, d), lambda i, j, e: (e, 0, 0))],
        out_specs=tspec(d),
        out_shape=jax.ShapeDtypeStruct((nb, r, d), F32),
        scratch_shapes=[pltpu.VMEM((bb * rt, d), F32)],
        compiler_params=_cparams(("arbitrary", "arbitrary", "arbitrary")),
        name="moe",
    )(h2, gate, x1, mod, wg, wu, wd)


def _rope_tables(pos):
    half = ROPE_DIM // 2
    inv = ROPE_THETA ** (-jnp.arange(half, dtype=F32) / half)
    ang = pos.astype(F32)[:, None] * inv[None, :]
    cos, sin = jnp.cos(ang), jnp.sin(ang)
    r = pos.shape[0]
    z = lambda n: jnp.zeros((r, n), F32)
    c64 = jnp.concatenate([cos, cos, jnp.ones((r, HEAD_DIM - ROPE_DIM), F32)], axis=1)
    sa64 = jnp.concatenate([-sin, z(HEAD_DIM - half)], axis=1)
    sb64 = jnp.concatenate([z(half), sin, z(HEAD_DIM - ROPE_DIM)], axis=1)
    iw_scale = jnp.full((r, N_IDX_HEADS), N_IDX_HEADS ** -0.5 * IDX_DIM ** -0.5, F32)
    ci = jnp.concatenate([c64, iw_scale, z(LANES - IDX_DIM - N_IDX_HEADS)], axis=1)
    two = lambda a: jnp.concatenate([a, a], axis=1)
    pad = lambda a: jnp.concatenate([a, z(LANES - HEAD_DIM)], axis=1)
    return jnp.stack([two(c64), two(sa64), two(sb64), ci, pad(sa64), pad(sb64)])


def _layer_weights(l, w_in, q_norm_g, k_norm_g, conv_w, w_out, w_router_group, b_router_group,
                   w_router_expert, b_router_expert, w_gate, w_up, w_down):
    d = D_MODEL
    w_cat = jnp.concatenate(
        [w_in[l][:, :IN_SPLIT], jnp.zeros((d, COL_A - IN_SPLIT), F32), w_in[l][:, IN_SPLIT:]], axis=1).astype(BF16)
    gqk = jnp.stack([jnp.tile(q_norm_g[l], 2), jnp.tile(k_norm_g[l], 2)])
    pad = ROUTER_ROWS - N_GROUPS - N_EXPERTS
    w_r = jnp.concatenate([w_router_group[l].T, w_router_expert[l].T, jnp.zeros((pad, d), F32)], axis=0).astype(BF16)
    b_r = jnp.concatenate([b_router_group[l], b_router_expert[l], jnp.zeros((pad,), F32)]).reshape(ROUTER_ROWS, 1)
    return dict(w_cat=w_cat, gqk=gqk, w_out=w_out[l].astype(BF16), w_r=w_r, b_r=b_r,
                wg=w_gate[l].astype(BF16), wu=w_up[l].astype(BF16), wd=w_down[l].astype(BF16))


def _trunk_layer(x, mod, tabs, past, lw, gn_mix, gn_ffn, conv_w, conv_b, ln_g, ln_b, consts,
                 *, bb, rt, tq, tt, mbb, mrt):
    nb, r, d = x.shape
    bd, tri = consts
    q, iq, kf, vf, kb, vb, ikwf, ikwb, g = _inproj_call(
        x, mod, gn_mix.reshape(1, d), lw["w_cat"], lw["gqk"], tabs, bd, bb=bb, rt=rt, tab_blocked=past is None,
        key_rows=rt if past is None else KEY_TILE)
    if past is None:
        key_ops, new_tiles, n_keys = (kb, vb, ikwb), None, r
        past32 = jnp.zeros((nb, CONV_HALO, D_CONV), F32)
        padded_tail = g[:, r - (CONV_W - 1):]
    else:
        k_past, v_past, ik_past, conv_past = past
        past_len = k_past.shape[1]
        assert past_len % KEY_TILE == 0 and r <= KEY_TILE
        n_keys = past_len + r
        key_ops = (k_past.astype(BF16).reshape(nb, past_len, LANES),
                   v_past.astype(BF16).reshape(nb, past_len, LANES), ik_past.astype(BF16))
        new_tiles = (kb, vb, ikwb)
        past32 = jnp.concatenate([jnp.zeros((nb, CONV_HALO - (CONV_W - 1), D_CONV), F32), conv_past], axis=1)
        padded_tail = jnp.concatenate([conv_past, g], axis=1)[:, -(CONV_W - 1):]
    attn = _attn_call(q, iq, ikwf, *key_ops, new_tiles, tri, tq=tq, n_keys=n_keys, causal=past is None)
    y = _conv_call(g, past32, conv_w, conv_b.reshape(1, -1), ln_g.reshape(1, -1), ln_b.reshape(1, -1), tt=tt)
    x1, h2, gate = _post_call(x, attn, y, mod, lw["w_out"], gn_ffn.reshape(1, d), lw["w_r"], lw["b_r"], bb=bb, rt=rt)
    x2 = _moe_call(h2, gate, x1, mod, lw["wg"], lw["wu"], lw["wd"], bb=mbb, rt=mrt)
    new_k = kf.reshape(nb, r, N_KV_HEADS, HEAD_DIM)
    new_v = vf.reshape(nb, r, N_KV_HEADS, HEAD_DIM)
    return x2, (new_k, new_v, ikwf[:, :, :IDX_DIM], padded_tail)


def kernel(x_prompt, x_sample, c_prompt, c_sample, cache_k, cache_v, cache_idx_k, state_conv, w_ada, b_ada, g_norm_mix, w_in, q_norm_g, k_norm_g, conv_w, conv_b, ln_conv_g, ln_conv_b, w_out, g_norm_ffn, w_router_group, b_router_group, w_router_expert, b_router_expert, w_gate, w_up, w_down):
    depth = w_ada.shape[0]
    bp, seq, d = x_prompt.shape
    bs, dseq, _ = x_sample.shape
    past_len = cache_k.shape[2]

    n_c = bp + bs
    n_cp = -(-n_c // SUBLANES) * SUBLANES
    c_all = jnp.concatenate([c_prompt, c_sample, jnp.zeros((n_cp - n_c, d), F32)], axis=0)
    mod_all = _ada_call(c_all, w_ada, b_ada)

    blk = np.arange(LANES) // HEAD_DIM
    bd = jnp.asarray(blk[:, None] == blk[None, :], BF16)
    tri = jnp.asarray(np.arange(LANES)[:, None] <= np.arange(LANES)[None, :], BF16)
    consts = (bd, tri)

    tabs_p = _rope_tables(jnp.arange(seq, dtype=jnp.int32))
    tabs_s = _rope_tables(past_len + jnp.arange(dseq, dtype=jnp.int32))

    xp, xs = x_prompt, x_sample
    st_p, st_s = [], []
    for l in range(depth):
        lw = _layer_weights(l, w_in, q_norm_g, k_norm_g, conv_w, w_out, w_router_group, b_router_group,
                            w_router_expert, b_router_expert, w_gate, w_up, w_down)
        common = (lw, g_norm_mix[l], g_norm_ffn[l], conv_w[l], conv_b[l], ln_conv_g[l], ln_conv_b[l], consts)
        mod_p = mod_all[l, :bp].reshape(bp, 1, 6 * d)
        mod_s = mod_all[l, bp:bp + bs].reshape(bs, 1, 6 * d)
        xp, st = _trunk_layer(xp, mod_p, tabs_p, None, *common,
                              bb=1, rt=512, tq=256, tt=256, mbb=1, mrt=1024)
        st_p.append(st)
        past = (cache_k[l], cache_v[l], cache_idx_k[l], state_conv[l])
        xs, st = _trunk_layer(xs, mod_s, tabs_s, past, *common,
                              bb=8, rt=dseq, tq=dseq, tt=dseq, mbb=16, mrt=dseq)
        st_s.append(st)

    stack = lambda sts, i: jnp.stack([s[i] for s in sts])
    return (xp, xs, stack(st_p, 0), stack(st_p, 1), stack(st_p, 2), stack(st_p, 3),
            stack(st_s, 0), stack(st_s, 1), stack(st_s, 2), stack(st_s, 3))
```

```python
import functools

import jax
import jax.numpy as jnp
import numpy as np
from jax import lax
from jax.experimental import pallas as pl
from jax.experimental.pallas import tpu as pltpu

D_MODEL = 1024
CHUNK = 64
N_HEADS = 8
HEAD_DIM = 64
N_KV_HEADS = 2
KV_GROUP = N_HEADS // N_KV_HEADS
ROPE_DIM = HEAD_DIM // 4
ROPE_THETA = 500000.0
N_IDX_HEADS = 8
IDX_DIM = 64
TOPK_MAX = 256
D_ATTN = N_HEADS * HEAD_DIM
D_CONV = 512
CONV_W = 31
N_GROUPS = 4
EXPERTS_PER_GROUP = 4
N_EXPERTS = N_GROUPS * EXPERTS_PER_GROUP
D_EXPERT = 256
MOE_EXPERTS_PER_STEP = 4
ROUTER_ROWS = 32
EPS = 1e-6

LANES = 128
SUBLANES = 8
KEY_TILE = 256
NEG_BIAS = -1e30
COUNT_UNIT = 64.0
COUNT_ROWS = 128
LOWEST = -3.0e38
SEL_MARK = 3.0e38
MAX_SAFE_SHIFT = 40.0
BOUND_SLACK = 1.01
INT_MIN = -(2 ** 31)
WORD_BITS = 32
SEARCH_CHUNKS = 4
RADIX_STEPS = (3,) * 10 + (2,)
VMEM_LIMIT = 48 * 1024 * 1024

COL_Q = 0
COL_K = 512
COL_V = 640
COL_IQ = 768
COL_IKW = 1280
COL_A = 1408
COL_B = 1920
W_CAT_COLS = 2432
IN_SPLIT = 1352

F32 = jnp.float32
BF16 = jnp.bfloat16
NT_DIMS = (((1,), (1,)), ((), ()))


def _cparams(sem):
    return pltpu.CompilerParams(dimension_semantics=sem, vmem_limit_bytes=VMEM_LIMIT)


def _ada_kernel(c_ref, w_ref, b_ref, o_ref):
    c = c_ref[...]
    s = (c * jax.nn.sigmoid(c)).astype(BF16)
    o_ref[0] = jnp.dot(s, w_ref[0].astype(BF16), preferred_element_type=F32) + b_ref[0]


def _ada_call(c_all, w_ada, b_ada):
    depth, d, n6 = w_ada.shape
    bp = c_all.shape[0]
    tn = 1536
    return pl.pallas_call(
        _ada_kernel,
        grid=(depth, n6 // tn),
        in_specs=[
            pl.BlockSpec((bp, d), lambda l, j: (0, 0)),
            pl.BlockSpec((1, d, tn), lambda l, j: (l, 0, j)),
            pl.BlockSpec((1, 1, tn), lambda l, j: (l, 0, j)),
        ],
        out_specs=pl.BlockSpec((1, bp, tn), lambda l, j: (l, 0, j)),
        out_shape=jax.ShapeDtypeStruct((depth, bp, n6), F32),
        compiler_params=_cparams(("arbitrary", "arbitrary")),
        name="ada_mod",
    )(c_all, w_ada, b_ada.reshape(depth, 1, n6))


def _head_mean_sq(y, bd):
    sq = y * y
    hi = sq.astype(BF16)
    lo = (sq - hi.astype(F32)).astype(BF16)
    s = jnp.dot(hi, bd, preferred_element_type=F32) + jnp.dot(lo, bd, preferred_element_type=F32)
    return s * (1.0 / HEAD_DIM)


def _rope(y, c, sa, sb):
    up = pltpu.roll(y, LANES - ROPE_DIM // 2, 1)
    dn = pltpu.roll(y, ROPE_DIM // 2, 1)
    return y * c + up * sa + dn * sb


def _inproj_kernel(x_ref, sh_ref, sc_ref, gn_ref, w_ref, gqk_ref, tabs_ref, bd_ref,
                   q_ref, iq_ref, kf_ref, vf_ref, kb_ref, vb_ref, ikwf_ref, ikwb_ref, g_ref,
                   *, bb, rt):
    m = bb * rt
    x = x_ref[...]
    ms = jnp.mean(x * x, axis=-1, keepdims=True)
    h = x * lax.rsqrt(ms + EPS) * gn_ref[...]
    h = h * (1.0 + sc_ref[...]) + sh_ref[...]
    hb = h.reshape(m, D_MODEL).astype(BF16)

    def tab(i):
        t = tabs_ref[i]
        if bb > 1:
            t = jnp.broadcast_to(t[None], (bb, rt, LANES)).reshape(m, LANES)
        return t

    cq, saq, sbq, ci, sai, sbi = (tab(i) for i in range(6))
    bd = bd_ref[...]
    gq = gqk_ref[0:1, :]
    gk = gqk_ref[1:2, :]

    def proj(c0, n):
        return jnp.dot(hb, w_ref[:, c0:c0 + n], preferred_element_type=F32)

    def put(ref, c0, val):
        ref[:, 0:rt, c0:c0 + val.shape[-1]] = val.reshape(bb, rt, val.shape[-1]).astype(ref.dtype)
        if ref.shape[1] > rt:
            ref[:, rt:, c0:c0 + val.shape[-1]] = jnp.zeros((bb, ref.shape[1] - rt, val.shape[-1]), ref.dtype)

    pq = proj(COL_Q, D_ATTN)
    for j in range(D_ATTN // LANES):
        y = pq[:, j * LANES:(j + 1) * LANES]
        y = y * lax.rsqrt(_head_mean_sq(y, bd) + EPS) * gq
        put(q_ref, j * LANES, _rope(y, cq, saq, sbq) * (HEAD_DIM ** -0.5))

    pkv = proj(COL_K, 2 * LANES)
    y = pkv[:, :LANES]
    y = y * lax.rsqrt(_head_mean_sq(y, bd) + EPS) * gk
    kk = _rope(y, cq, saq, sbq)
    put(kf_ref, 0, kk)
    put(kb_ref, 0, kk)
    vv = pkv[:, LANES:]
    put(vf_ref, 0, vv)
    put(vb_ref, 0, vv)

    piq = proj(COL_IQ, N_IDX_HEADS * IDX_DIM)
    for j in range(N_IDX_HEADS * IDX_DIM // LANES):
        put(iq_ref, j * LANES, _rope(piq[:, j * LANES:(j + 1) * LANES], cq, saq, sbq))

    ikw = _rope(proj(COL_IKW, LANES), ci, sai, sbi)
    put(ikwf_ref, 0, ikw)
    put(ikwb_ref, 0, ikw)

    pu = proj(COL_A, 2 * D_CONV)
    put(g_ref, 0, pu[:, :D_CONV] * jax.nn.sigmoid(pu[:, D_CONV:]))


def _inproj_call(x, mod, gn, w_cat, gqk, tabs, bd, *, bb, rt, tab_blocked, key_rows):
    nb, r, d = x.shape
    grid = (nb // bb, r // rt)
    assert key_rows == rt or r == rt
    tok = lambda n, dt: jax.ShapeDtypeStruct((nb, r, n), dt)
    tspec = lambda n: pl.BlockSpec((bb, rt, n), lambda i, j: (i, j, 0))
    ktok = lambda n: jax.ShapeDtypeStruct((nb, r // rt * key_rows, n), BF16)
    kspec = lambda n: pl.BlockSpec((bb, key_rows, n), lambda i, j: (i, j, 0))
    return pl.pallas_call(
        functools.partial(_inproj_kernel, bb=bb, rt=rt),
        grid=grid,
        in_specs=[
            tspec(d),
            pl.BlockSpec((bb, 1, d), lambda i, j: (i, 0, 0)),
            pl.BlockSpec((bb, 1, d), lambda i, j: (i, 0, 1)),
            pl.BlockSpec((1, d), lambda i, j: (0, 0)),
            pl.BlockSpec((d, W_CAT_COLS), lambda i, j: (0, 0)),
            pl.BlockSpec((2, LANES), lambda i, j: (0, 0)),
            pl.BlockSpec((6, rt, LANES), (lambda i, j: (0, j, 0)) if tab_blocked else (lambda i, j: (0, 0, 0))),
            pl.BlockSpec((LANES, LANES), lambda i, j: (0, 0)),
        ],
        out_specs=[tspec(D_ATTN), tspec(D_ATTN), tspec(LANES), tspec(LANES), kspec(LANES),
                   kspec(LANES), tspec(LANES), kspec(LANES), tspec(D_CONV)],
        out_shape=[tok(D_ATTN, BF16), tok(D_ATTN, BF16), tok(LANES, F32), tok(LANES, F32),
                   ktok(LANES), ktok(LANES), tok(LANES, F32), ktok(LANES),
                   tok(D_CONV, F32)],
        compiler_params=_cparams(("arbitrary", "arbitrary")),
        name="in_proj",
    )(x, mod, mod, gn, w_cat, gqk, tabs, bd)


def _stack_heads(ref, tq, heads):
    lane = lax.broadcasted_iota(jnp.int32, (tq, LANES), 1)
    low = lane < HEAD_DIM
    out = []
    for h, half in heads:
        slab = ref[0, :, (h // 2) * LANES:(h // 2 + 1) * LANES].astype(F32)
        if (h % 2) != half:
            slab = pltpu.roll(slab, HEAD_DIM, 1)
        keep = low if half == 0 else jnp.logical_not(low)
        out.append(jnp.where(keep, slab, 0.0).astype(BF16))
    return jnp.concatenate(out, axis=0)


def _tile_loop(nt, body, init):
    def run(t0, n, carry):
        for u in range(n):
            carry = body(t0 + u, carry)
        return carry

    if isinstance(nt, int):
        carry = lax.fori_loop(0, nt // 4, lambda p, c: run(4 * p, 4, c), init)
        return run(nt - nt % 4, nt % 4, carry)
    carry = lax.fori_loop(0, lax.shift_right_logical(nt, 2), lambda p, c: run(4 * p, 4, c), init)
    t2 = nt & ~3
    carry = lax.cond((nt & 2) == 2, lambda c: run(t2, 2, c), lambda c: c, carry)
    t1 = nt & ~1
    return lax.cond((nt & 1) == 1, lambda c: run(t1, 1, c), lambda c: c, carry)


def _ukey(s):
    b = lax.bitcast_convert_type(s, jnp.int32)
    return b ^ (lax.shift_right_arithmetic(b, WORD_BITS - 1) | jnp.int32(INT_MIN))


def _bit_transpose(a):
    a = list(a)
    for j, mask in ((16, 0x0000FFFF), (8, 0x00FF00FF), (4, 0x0F0F0F0F), (2, 0x33333333), (1, 0x55555555)):
        k = 0
        while k < WORD_BITS:
            t = (a[k] ^ lax.shift_right_logical(a[k + j], j)) & jnp.int32(mask)
            a[k] = a[k] ^ t
            a[k + j] = a[k + j] ^ lax.shift_left(t, j)
            k = (k + j + 1) & ~j
    return a


def _attn_kernel(*refs, tq, n_keys, causal, ktop, has_new):
    q_ref, iq_ref, ikwf_ref, k_ref, v_ref, ik_ref = refs[:6]
    new_refs = refs[6:9] if has_new else None
    tri_ref, o_ref, sc_ref, m_ref, acc_ref, kn_ref, st_ref, pa_ref = refs[6 + 3 * has_new:14 + 3 * has_new]
    pb_refs = refs[14 + 3 * has_new:]
    i = pl.program_id(1)
    sub = KEY_TILE // LANES
    n_main = k_ref.shape[1] // KEY_TILE
    if causal:
        nt = lax.shift_right_logical(i * tq + tq + KEY_TILE - 1, 8)
    else:
        nt = n_main
    nt_all = nt + 1 if has_new else nt

    def key_tile(which, t):
        if has_new and isinstance(t, int) and t == n_main:
            return new_refs[which][0]
        ref = (k_ref, v_ref, ik_ref)[which]
        return ref[0, pl.ds(pl.multiple_of(t * KEY_TILE, KEY_TILE), KEY_TILE), :]

    def all_tiles(body, init):
        carry = _tile_loop(nt, body, init)
        return body(n_main, carry) if has_new else carry

    rows = lax.broadcasted_iota(jnp.int32, (tq, LANES), 0)
    lanes = lax.broadcasted_iota(jnp.int32, (tq, LANES), 1)

    iqs = _stack_heads(iq_ref, tq, [(h, 0) for h in range(N_IDX_HEADS)])
    w8 = ikwf_ref[0][:, IDX_DIM:IDX_DIM + N_IDX_HEADS]
    wb = [jnp.broadcast_to(w8[:, h:h + 1], (tq, LANES)) for h in range(N_IDX_HEADS)]

    def score_body(t, carry):
        kt = key_tile(2, t)
        d = lax.dot_general(iqs[:, :kt.shape[1]], kt, NT_DIMS, preferred_element_type=F32)
        for c in range(sub):
            s = jnp.zeros((tq, LANES), F32)
            for h in range(N_IDX_HEADS):
                s = s + jnp.maximum(d[h * tq:(h + 1) * tq, c * LANES:(c + 1) * LANES], 0.0) * wb[h]
            kpos = t * KEY_TILE + c * LANES + lanes
            if causal:
                adm = lax.shift_right_logical(kpos, 6) <= lax.shift_right_logical(i * tq + rows, 6)
            else:
                adm = kpos < n_keys
            sc_ref[t * sub + c] = jnp.where(adm, jnp.where(s == 0.0, 0.0, s), -jnp.inf)
        return carry

    all_tiles(score_body, 0)

    def fill_body(j, carry):
        sc_ref[j] = jnp.full((tq, LANES), -jnp.inf, F32)
        return carry

    lax.fori_loop(nt_all * sub, WORD_BITS, fill_body, 0)

    def planes_body(r, carry):
        rs = pl.ds(pl.multiple_of(r * SUBLANES, SUBLANES), SUBLANES)
        words = _bit_transpose([_ukey(sc_ref[j, rs, :]) for j in range(WORD_BITS)])
        for p in range(WORD_BITS):
            pa_ref[p, rs, :] = words[p]
        for e, pb_ref in enumerate(pb_refs):
            u = _ukey(sc_ref[WORD_BITS + e, rs, :])
            for p in range(WORD_BITS):
                pb_ref[p, rs, :] = lax.shift_right_logical(u, WORD_BITS - 1 - p) & 1
        return carry

    lax.fori_loop(0, tq // SUBLANES, planes_body, 0)

    plane_refs = (pa_ref,) + tuple(pb_refs)
    ones_m = jnp.ones((LANES, LANES), BF16)

    rc = tq // SEARCH_CHUNKS

    def lane_total(parts):
        x = jnp.concatenate([p.astype(F32).astype(BF16) for p in parts], axis=0)
        tot = jnp.dot(x, ones_m, preferred_element_type=F32)
        return [tot[n * rc:(n + 1) * rc] for n in range(len(parts))]

    def popcount_sum(words):
        return sum(lax.population_count(w) for w in words)

    eqs = [[jnp.full((rc, LANES), -1, jnp.int32) for _ in plane_refs] for _ in range(SEARCH_CHUNKS)]
    cnts = [jnp.zeros((rc, LANES), F32) for _ in range(SEARCH_CHUNKS)]
    ukeys = [jnp.zeros((rc, LANES), jnp.int32) for _ in range(SEARCH_CHUNKS)]
    plane0 = 0
    for nbits in RADIX_STEPS:
        for ch in range(SEARCH_CHUNKS):
            r0 = ch * rc
            leaves = [list(eqs[ch])]
            for b in range(nbits):
                planes = [pref[plane0 + b, r0:r0 + rc, :] for pref in plane_refs]
                split = []
                for words in leaves:
                    ones = [w & p for w, p in zip(words, planes)]
                    split += [ones, [w ^ o for w, o in zip(words, ones)]]
                leaves = split
            totals = lane_total([popcount_sum(words) for words in leaves[:-1]])
            seen = [cnts[ch]]
            for tot in totals:
                seen.append(seen[-1] + tot)
            decided = [s >= ktop for s in seen[1:]]
            new_eq, new_cnt, digit = leaves[-1], seen[-1], 0
            for n in reversed(range(len(decided))):
                new_eq = [jnp.where(decided[n], a, c) for a, c in zip(leaves[n], new_eq)]
                new_cnt = jnp.where(decided[n], seen[n], new_cnt)
                digit = jnp.where(decided[n], len(decided) - n, digit)
            eqs[ch], cnts[ch] = new_eq, new_cnt
            ukeys[ch] = ukeys[ch] | lax.shift_left(digit, WORD_BITS - plane0 - nbits)
        plane0 += nbits
    ukey = jnp.concatenate(ukeys, axis=0)

    bits = jnp.where(ukey < 0, ukey ^ jnp.int32(INT_MIN), ~ukey)
    guess = lax.bitcast_convert_type(bits, F32)[:, 0:1]
    guess = jnp.where(guess == -jnp.inf, LOWEST, guess)

    def counts(thr):
        rb = min(tq, COUNT_ROWS)
        accs = []
        for r0 in range(0, tq, rb):
            thr_b = jnp.broadcast_to(thr[r0:r0 + rb], (rb, LANES))

            def body(t, acc, r0=r0, thr_b=thr_b):
                for c in range(sub):
                    s = sc_ref[t * sub + c, r0:r0 + rb, :]
                    acc = acc + jnp.where(s > thr_b, COUNT_UNIT + 1.0, jnp.where(s >= thr_b, 1.0, 0.0))
                return acc

            accs.append(all_tiles(body, jnp.zeros((rb, LANES), F32)))
        acc = jnp.concatenate(accs, axis=0)
        gt = jnp.floor(acc * (1.0 / COUNT_UNIT))
        ge = acc - COUNT_UNIT * gt
        return jnp.sum(gt, axis=1, keepdims=True), jnp.sum(ge, axis=1, keepdims=True)

    def keep(thr, n_gt, n_ge):
        for n, v in enumerate((thr, n_gt, n_ge)):
            st_ref[n] = jnp.broadcast_to(v, (tq, LANES))

    n_gt0, n_ge0 = counts(guess)
    is_kth = ((n_gt0 < ktop) & (n_ge0 >= ktop)) | ((guess == LOWEST) & (n_ge0 < ktop))
    confirmed = jnp.min(jnp.where(is_kth, 1.0, 0.0)) > 0.5
    keep(guess, n_gt0, n_ge0)

    @pl.when(jnp.logical_not(confirmed))
    def _():
        def key_to_float(key):
            return lax.bitcast_convert_type(jnp.where(key >= 0, key, key ^ jnp.int32(0x7FFFFFFF)), F32)

        def bit_body(b, key):
            cand = key + lax.shift_left(jnp.int32(1), WORD_BITS - 1 - b)
            return jnp.where(counts(key_to_float(cand))[1] >= ktop, cand, key)

        key = lax.fori_loop(0, WORD_BITS, bit_body, jnp.full((tq, 1), INT_MIN, jnp.int32))
        thr = jnp.where(key == INT_MIN, LOWEST, key_to_float(key))
        keep(thr, *counts(thr))

    thr_b = st_ref[0]
    n_gt = st_ref[1]
    has_ties = jnp.max(jnp.where(thr_b == LOWEST, 0.0, st_ref[2])) > ktop

    @pl.when(has_ties)
    def _():
        tri = tri_ref[...]
        need_b = jnp.broadcast_to(ktop - n_gt, (tq, LANES))

        def tie_body(j, seen):
            s = sc_ref[j]
            eq = s == thr_b
            pre = jnp.dot(jnp.where(eq, 1.0, 0.0).astype(BF16), tri, preferred_element_type=F32)
            sel = (s > thr_b) | (eq & (pre + seen <= need_b))
            sc_ref[j] = jnp.where(sel, SEL_MARK, -SEL_MARK)
            return seen + jnp.broadcast_to(pre[:, LANES - 1:LANES], (tq, LANES))

        lax.fori_loop(0, nt_all * sub, tie_body, jnp.zeros((tq, LANES), F32))

    sel_thr = jnp.where(has_ties, 0.0, thr_b)

    def bias(t, c):
        return jnp.where(sc_ref[t * sub + c] >= sel_thr, 0.0, NEG_BIAS)

    qs = [_stack_heads(q_ref, tq, [(g * KV_GROUP + hh, g) for hh in range(KV_GROUP)])
          for g in range(N_KV_HEADS)]
    acc_ref[...] = jnp.zeros(acc_ref.shape, F32)
    sq_rows = lax.broadcasted_iota(jnp.int32, (LANES, LANES), 0)
    sq_lanes = lax.broadcasted_iota(jnp.int32, (LANES, LANES), 1)
    head_block = (lax.shift_right_logical(sq_rows, 6) == lax.shift_right_logical(sq_lanes, 6)).astype(BF16)

    @pl.when(i == 0)
    def _():
        def kn_body(t, mx):
            kt = key_tile(0, t).astype(F32)
            return jnp.maximum(mx, jnp.dot((kt * kt).astype(BF16), head_block, preferred_element_type=F32))

        mx = _tile_loop(n_main, kn_body, jnp.zeros((KEY_TILE, LANES), F32))
        if has_new:
            mx = kn_body(n_main, mx)
        kn_ref[...] = jnp.broadcast_to(jnp.max(mx, axis=0, keepdims=True), kn_ref.shape)

    kn = kn_ref[0:1, :]
    kn_other = pltpu.roll(kn, HEAD_DIM, 1)
    low_half = lanes[0:1, :] < HEAD_DIM
    bounds = []
    for g in range(N_KV_HEADS):
        qf = qs[g].astype(F32)
        kn_g = jnp.where(low_half, kn, kn_other) if g == 0 else jnp.where(low_half, kn_other, kn)
        qn = jnp.dot((qf * qf).astype(BF16), ones_m, preferred_element_type=F32)
        bounds.append(jnp.sqrt(qn * kn_g) * BOUND_SLACK)
    bounded = jnp.maximum(jnp.max(bounds[0]), jnp.max(bounds[1])) < MAX_SAFE_SHIFT

    def logits(t, g):
        return lax.dot_general(qs[g], key_tile(0, t), NT_DIMS, preferred_element_type=F32)

    @pl.when(bounded)
    def _():
        for g in range(N_KV_HEADS):
            m_ref[g] = bounds[g]

    @pl.when(jnp.logical_not(bounded))
    def _():
        m_ref[...] = jnp.full(m_ref.shape, NEG_BIAS, F32)

        def max_body(t, carry):
            bs = [bias(t, c) for c in range(sub)]
            for g in range(N_KV_HEADS):
                lg = logits(t, g)
                for hh in range(KV_GROUP):
                    r0 = hh * tq
                    mm = m_ref[g, r0:r0 + tq]
                    for c in range(sub):
                        mm = jnp.maximum(mm, lg[r0:r0 + tq, c * LANES:(c + 1) * LANES] + bs[c])
                    m_ref[g, r0:r0 + tq] = mm
            return carry

        all_tiles(max_body, 0)
        for g in range(N_KV_HEADS):
            mrow = jnp.max(m_ref[g], axis=1, keepdims=True)
            m_ref[g] = jnp.broadcast_to(mrow, m_ref.shape[1:])

    def pv_body(t, carry):
        bs = [bias(t, c) for c in range(sub)]
        for g in range(N_KV_HEADS):
            lg = logits(t, g)
            ps = []
            for hh in range(KV_GROUP):
                r0 = hh * tq
                mm = m_ref[g, r0:r0 + tq]
                ps.append(jnp.concatenate(
                    [jnp.exp(lg[r0:r0 + tq, c * LANES:(c + 1) * LANES] + bs[c] - mm)
                     for c in range(sub)], axis=1).astype(BF16))
            p = jnp.concatenate(ps, axis=0)
            va = jnp.where(own_half[g], key_tile(1, t), jnp.ones((), BF16))
            acc_ref[g] += jnp.dot(p, va, preferred_element_type=F32)
        return carry

    key_lanes = lax.broadcasted_iota(jnp.int32, (KEY_TILE, LANES), 1)
    own_half = [(key_lanes < HEAD_DIM) == (g == 0) for g in range(N_KV_HEADS)]
    all_tiles(pv_body, 0)

    outs = []
    for h in range(N_HEADS):
        g, hh = divmod(h, KV_GROUP)
        a = acc_ref[g, hh * tq:(hh + 1) * tq]
        num = a[:, g * HEAD_DIM:(g + 1) * HEAD_DIM]
        den = a[:, (1 - g) * HEAD_DIM:(1 - g) * HEAD_DIM + 1]
        outs.append(num / den)
    o_ref[0] = jnp.concatenate(outs, axis=1).astype(o_ref.dtype)


def _attn_call(q, iq, ikwf, k_main, v_main, ik_main, new_tiles, tri, *, tq, n_keys, causal):
    b, t, _ = q.shape
    s_main = k_main.shape[1]
    has_new = new_tiles is not None
    s_pad = s_main + (KEY_TILE if has_new else 0)
    ktop = min(TOPK_MAX, n_keys // 4)
    n_extra = max(-(-n_keys // LANES) - WORD_BITS, 0)
    qspec = lambda n: pl.BlockSpec((1, tq, n), lambda bi, i: (bi, i, 0))
    kspec = lambda a: pl.BlockSpec((1,) + a.shape[1:], lambda bi, i: (bi, 0, 0))
    key_ops = [k_main, v_main, ik_main] + (list(new_tiles) if has_new else [])
    return pl.pallas_call(
        functools.partial(_attn_kernel, tq=tq, n_keys=n_keys, causal=causal, ktop=ktop, has_new=has_new),
        grid=(b, t // tq),
        in_specs=[qspec(D_ATTN), qspec(D_ATTN), qspec(LANES)] + [kspec(a) for a in key_ops]
                 + [pl.BlockSpec((LANES, LANES), lambda bi, i: (0, 0))],
        out_specs=qspec(D_ATTN),
        out_shape=jax.ShapeDtypeStruct((b, t, D_ATTN), BF16),
        scratch_shapes=[
            pltpu.VMEM((max(s_pad // LANES, WORD_BITS), tq, LANES), F32),
            pltpu.VMEM((N_KV_HEADS, KV_GROUP * tq, LANES), F32),
            pltpu.VMEM((N_KV_HEADS, KV_GROUP * tq, LANES), F32),
            pltpu.VMEM((SUBLANES, LANES), F32),
            pltpu.VMEM((3, tq, LANES), F32),
        ] + [pltpu.VMEM((WORD_BITS, tq, LANES), jnp.int32)] * (1 + n_extra),
        compiler_params=_cparams(("arbitrary", "arbitrary")),
        name="dsa_attention",
    )(q, iq, ikwf, *key_ops, tri)


CONV_HALO = 32
CONV_ROWS = 32


def _conv_kernel(g_ref, past_ref, w_ref, b_ref, lg_ref, lb_ref, y_ref, win_ref, *, tt):
    j = pl.program_id(1)

    @pl.when(j == 0)
    def _():
        win_ref[0, 0:CONV_HALO] = past_ref[0]

    @pl.when(j > 0)
    def _():
        win_ref[0, 0:CONV_HALO] = win_ref[0, tt:tt + CONV_HALO]

    win_ref[0, CONV_HALO:CONV_HALO + tt] = g_ref[0]
    rows = tt + CONV_HALO - SUBLANES
    for b in range(1, SUBLANES):
        win_ref[b, 0:rows] = win_ref[0, b:b + rows]
    off = CONV_HALO - (CONV_W - 1)
    bias = b_ref[...]
    lg = lg_ref[...]
    lb = lb_ref[...]
    for r0 in range(0, tt, CONV_ROWS):
        acc = jnp.zeros((CONV_ROWS // SUBLANES, SUBLANES, D_CONV), F32)
        for tap in range(CONV_W):
            a0, b = divmod(tap + off, SUBLANES)
            rows_tap = win_ref[b, r0 + a0 * SUBLANES:r0 + a0 * SUBLANES + CONV_ROWS]
            acc = acc + rows_tap.reshape(acc.shape) * w_ref[tap][None]
        acc = acc.reshape(CONV_ROWS, D_CONV)
        acc = acc + bias
        mu = jnp.mean(acc, axis=-1, keepdims=True)
        xc = acc - mu
        var = jnp.mean(xc * xc, axis=-1, keepdims=True)
        z = xc * lax.rsqrt(var + EPS) * lg + lb
        y_ref[0, r0:r0 + CONV_ROWS] = (z * jax.nn.sigmoid(z)).astype(y_ref.dtype)


def _conv_call(g, past32, conv_w, conv_b, ln_g, ln_b, *, tt):
    nb, r, _ = g.shape
    vec = pl.BlockSpec((1, D_CONV), lambda bi, j: (0, 0))
    return pl.pallas_call(
        functools.partial(_conv_kernel, tt=tt),
        grid=(nb, r // tt),
        in_specs=[
            pl.BlockSpec((1, tt, D_CONV), lambda bi, j: (bi, j, 0)),
            pl.BlockSpec((1, CONV_HALO, D_CONV), lambda bi, j: (bi, 0, 0)),
            pl.BlockSpec((CONV_W, SUBLANES, D_CONV), lambda bi, j: (0, 0, 0)),
            vec, vec, vec,
        ],
        out_specs=pl.BlockSpec((1, tt, D_CONV), lambda bi, j: (bi, j, 0)),
        out_shape=jax.ShapeDtypeStruct((nb, r, D_CONV), BF16),
        scratch_shapes=[pltpu.VMEM((SUBLANES, tt + CONV_HALO, D_CONV), F32)],
        compiler_params=_cparams(("arbitrary", "arbitrary")),
        name="conv_module",
    )(g, past32, jnp.broadcast_to(conv_w[:, None, :], (CONV_W, SUBLANES, D_CONV)), conv_b, ln_g, ln_b)


def _first_argmax(rows):
    mx = functools.reduce(jnp.maximum, rows)
    first = len(rows) - 1
    for n in reversed(range(len(rows) - 1)):
        first = jnp.where(rows[n] == mx, n, first)
    return mx, first


def _post_kernel(x_ref, at_ref, y_ref, gta_ref, shf_ref, scf_ref, wo_ref, gn_ref, wr_ref, br_ref,
                 x1_ref, h2_ref, gate_ref, gt_ref, *, bb, rt):
    m = bb * rt
    at = at_ref[...].reshape(m, D_ATTN)
    yy = y_ref[...].reshape(m, D_CONV)
    mixo = (jnp.dot(at, wo_ref[0:D_ATTN], preferred_element_type=F32)
            + jnp.dot(yy, wo_ref[D_ATTN:D_ATTN + D_CONV], preferred_element_type=F32))
    x1 = x_ref[...] + gta_ref[...] * mixo.reshape(bb, rt, D_MODEL)
    x1_ref[...] = x1
    ms = jnp.mean(x1 * x1, axis=-1, keepdims=True)
    h = x1 * lax.rsqrt(ms + EPS) * gn_ref[...]
    h = h * (1.0 + scf_ref[...]) + shf_ref[...]
    hb = h.reshape(m, D_MODEL).astype(BF16)
    h2_ref[...] = hb.reshape(bb, rt, D_MODEL)

    lt = lax.dot_general(wr_ref[...], hb, NT_DIMS, preferred_element_type=F32) + br_ref[...]
    row = lambda n: lt[n:n + 1, :]
    gmax, gsel = _first_argmax([row(g) for g in range(N_GROUPS)])
    p_g = 1.0 / sum(jnp.exp(row(g) - gmax) for g in range(N_GROUPS))
    ein = []
    for e in range(EXPERTS_PER_GROUP):
        v = row(N_GROUPS + (N_GROUPS - 1) * EXPERTS_PER_GROUP + e)
        for g in reversed(range(N_GROUPS - 1)):
            v = jnp.where(gsel == g, row(N_GROUPS + g * EXPERTS_PER_GROUP + e), v)
        ein.append(v)
    e1, i1 = _first_argmax(ein)
    e2, i2 = _first_argmax([jnp.where(i1 == e, -jnp.inf, ein[e]) for e in range(EXPERTS_PER_GROUP)])
    r2 = jnp.exp(e2 - e1)
    w1 = p_g / (1.0 + r2)
    w2 = p_g * r2 / (1.0 + r2)
    gt_ref[...] = jnp.zeros(gt_ref.shape, F32)
    for g in range(N_GROUPS):
        for e in range(EXPERTS_PER_GROUP):
            n = g * EXPERTS_PER_GROUP + e
            gt_ref[n:n + 1, :] = jnp.where(gsel == g, jnp.where(i1 == e, w1, 0.0) + jnp.where(i2 == e, w2, 0.0), 0.0)
    gate_ref[...] = gt_ref[...].T.reshape(bb, rt, LANES)


def _post_call(x, attn, y, mod, w_out, gn, w_r, b_r, *, bb, rt):
    nb, r, d = x.shape
    tspec = lambda n: pl.BlockSpec((bb, rt, n), lambda i, j: (i, j, 0))
    mspec = lambda c: pl.BlockSpec((bb, 1, d), lambda i, j: (i, 0, c))
    full = lambda a, b_: pl.BlockSpec((a, b_), lambda i, j: (0, 0))
    return pl.pallas_call(
        functools.partial(_post_kernel, bb=bb, rt=rt),
        grid=(nb // bb, r // rt),
        in_specs=[tspec(d), tspec(D_ATTN), tspec(D_CONV), mspec(2), mspec(3), mspec(4),
                  full(D_ATTN + D_CONV, d), full(1, d), full(ROUTER_ROWS, d), full(ROUTER_ROWS, 1)],
        out_specs=[tspec(d), tspec(d), tspec(LANES)],
        out_shape=[jax.ShapeDtypeStruct((nb, r, d), F32), jax.ShapeDtypeStruct((nb, r, d), BF16),
                   jax.ShapeDtypeStruct((nb, r, LANES), F32)],
        scratch_shapes=[pltpu.VMEM((LANES, bb * rt), F32)],
        compiler_params=_cparams(("arbitrary", "arbitrary")),
        name="out_proj_router",
    )(x, attn, y, mod, mod, mod, w_out, gn, w_r, b_r)


def _moe_kernel(h_ref, gate_ref, x1_ref, gtf_ref, wg_ref, wu_ref, wd_ref, o_ref, acc_ref, *, bb, rt):
    e = pl.program_id(2)
    m = bb * rt

    @pl.when(e == 0)
    def _():
        acc_ref[...] = jnp.zeros(acc_ref.shape, F32)

    hb = h_ref[...].reshape(m, D_MODEL)
    gate = gate_ref[...].reshape(m, LANES)
    lane = lax.broadcasted_iota(jnp.int32, (m, LANES), 1)
    out = jnp.zeros((m, D_MODEL), F32)
    for n in range(MOE_EXPERTS_PER_STEP):
        ge = jnp.sum(jnp.where(lane == e * MOE_EXPERTS_PER_STEP + n, gate, 0.0), axis=1, keepdims=True)
        a = jnp.dot(hb, wg_ref[n], preferred_element_type=F32)
        u = jnp.dot(hb, wu_ref[n], preferred_element_type=F32)
        hid = (a * jax.nn.sigmoid(a)) * u * ge
        out = out + jnp.dot(hid.astype(BF16), wd_ref[n], preferred_element_type=F32)
    acc_ref[...] += out

    @pl.when(e == N_EXPERTS // MOE_EXPERTS_PER_STEP - 1)
    def _():
        o_ref[...] = x1_ref[...] + gtf_ref[...] * acc_ref[...].reshape(bb, rt, D_MODEL)


def _moe_call(h2, gate, x1, mod, wg, wu, wd, *, bb, rt):
    nb, r, d = x1.shape
    tspec = lambda n: pl.BlockSpec((bb, rt, n), lambda i, j, e: (i, j, 0))
    return pl.pallas_call(
        functools.partial(_moe_kernel, bb=bb, rt=rt),
        grid=(nb // bb, r // rt, N_EXPERTS // MOE_EXPERTS_PER_STEP),
        in_specs=[tspec(d), tspec(LANES), tspec(d),
                  pl.BlockSpec((bb, 1, d), lambda i, j, e: (i, 0, 5)),
                  pl.BlockSpec((MOE_EXPERTS_PER_STEP, d, D_EXPERT), lambda i, j, e: (e, 0, 0)),
                  pl.BlockSpec((MOE_EXPERTS_PER_STEP, d, D_EXPERT), lambda i, j, e: (e, 0, 0)),
                  pl.BlockSpec((MOE_EXPERTS_PER_STEP, D_EXPERT, d), lambda i, j, e: (e, 0, 0))],
        out_specs=tspec(d),
        out_shape=jax.ShapeDtypeStruct((nb, r, d), F32),
        scratch_shapes=[pltpu.VMEM((bb * rt, d), F32)],
        compiler_params=_cparams(("arbitrary", "arbitrary", "arbitrary")),
        name="moe",
    )(h2, gate, x1, mod, wg, wu, wd)


def _rope_tables(pos):
    half = ROPE_DIM // 2
    inv = ROPE_THETA ** (-jnp.arange(half, dtype=F32) / half)
    ang = pos.astype(F32)[:, None] * inv[None, :]
    cos, sin = jnp.cos(ang), jnp.sin(ang)
    r = pos.shape[0]
    z = lambda n: jnp.zeros((r, n), F32)
    c64 = jnp.concatenate([cos, cos, jnp.ones((r, HEAD_DIM - ROPE_DIM), F32)], axis=1)
    sa64 = jnp.concatenate([-sin, z(HEAD_DIM - half)], axis=1)
    sb64 = jnp.concatenate([z(half), sin, z(HEAD_DIM - ROPE_DIM)], axis=1)
    iw_scale = jnp.full((r, N_IDX_HEADS), N_IDX_HEADS ** -0.5 * IDX_DIM ** -0.5, F32)
    ci = jnp.concatenate([c64, iw_scale, z(LANES - IDX_DIM - N_IDX_HEADS)], axis=1)
    two = lambda a: jnp.concatenate([a, a], axis=1)
    pad = lambda a: jnp.concatenate([a, z(LANES - HEAD_DIM)], axis=1)
    return jnp.stack([two(c64), two(sa64), two(sb64), ci, pad(sa64), pad(sb64)])


def _layer_weights(l, w_in, q_norm_g, k_norm_g, conv_w, w_out, w_router_group, b_router_group,
                   w_router_expert, b_router_expert, w_gate, w_up, w_down):
    d = D_MODEL
    w_cat = jnp.concatenate(
        [w_in[l][:, :IN_SPLIT], jnp.zeros((d, COL_A - IN_SPLIT), F32), w_in[l][:, IN_SPLIT:]], axis=1).astype(BF16)
    gqk = jnp.stack([jnp.tile(q_norm_g[l], 2), jnp.tile(k_norm_g[l], 2)])
    pad = ROUTER_ROWS - N_GROUPS - N_EXPERTS
    w_r = jnp.concatenate([w_router_group[l].T, w_router_expert[l].T, jnp.zeros((pad, d), F32)], axis=0).astype(BF16)
    b_r = jnp.concatenate([b_router_group[l], b_router_expert[l], jnp.zeros((pad,), F32)]).reshape(ROUTER_ROWS, 1)
    return dict(w_cat=w_cat, gqk=gqk, w_out=w_out[l].astype(BF16), w_r=w_r, b_r=b_r,
                wg=w_gate[l].astype(BF16), wu=w_up[l].astype(BF16), wd=w_down[l].astype(BF16))


def _trunk_layer(x, mod, tabs, past, lw, gn_mix, gn_ffn, conv_w, conv_b, ln_g, ln_b, consts,
                 *, bb, rt, tq, tt, mbb, mrt):
    nb, r, d = x.shape
    bd, tri = consts
    q, iq, kf, vf, kb, vb, ikwf, ikwb, g = _inproj_call(
        x, mod, gn_mix.reshape(1, d), lw["w_cat"], lw["gqk"], tabs, bd, bb=bb, rt=rt, tab_blocked=past is None,
        key_rows=rt if past is None else KEY_TILE)
    if past is None:
        key_ops, new_tiles, n_keys = (kb, vb, ikwb), None, r
        past32 = jnp.zeros((nb, CONV_HALO, D_CONV), F32)
        padded_tail = g[:, r - (CONV_W - 1):]
    else:
        k_past, v_past, ik_past, conv_past = past
        past_len = k_past.shape[1]
        assert past_len % KEY_TILE == 0 and r <= KEY_TILE
        n_keys = past_len + r
        key_ops = (k_past.astype(BF16).reshape(nb, past_len, LANES),
                   v_past.astype(BF16).reshape(nb, past_len, LANES), ik_past.astype(BF16))
        new_tiles = (kb, vb, ikwb)
        past32 = jnp.concatenate([jnp.zeros((nb, CONV_HALO - (CONV_W - 1), D_CONV), F32), conv_past], axis=1)
        padded_tail = jnp.concatenate([conv_past, g], axis=1)[:, -(CONV_W - 1):]
    attn = _attn_call(q, iq, ikwf, *key_ops, new_tiles, tri, tq=tq, n_keys=n_keys, causal=past is None)
    y = _conv_call(g, past32, conv_w, conv_b.reshape(1, -1), ln_g.reshape(1, -1), ln_b.reshape(1, -1), tt=tt)
    x1, h2, gate = _post_call(x, attn, y, mod, lw["w_out"], gn_ffn.reshape(1, d), lw["w_r"], lw["b_r"], bb=bb, rt=rt)
    x2 = _moe_call(h2, gate, x1, mod, lw["wg"], lw["wu"], lw["wd"], bb=mbb, rt=mrt)
    new_k = kf.reshape(nb, r, N_KV_HEADS, HEAD_DIM)
    new_v = vf.reshape(nb, r, N_KV_HEADS, HEAD_DIM)
    return x2, (new_k, new_v, ikwf[:, :, :IDX_DIM], padded_tail)


def kernel(x_prompt, x_sample, c_prompt, c_sample, cache_k, cache_v, cache_idx_k, state_conv, w_ada, b_ada, g_norm_mix, w_in, q_norm_g, k_norm_g, conv_w, conv_b, ln_conv_g, ln_conv_b, w_out, g_norm_ffn, w_router_group, b_router_group, w_router_expert, b_router_expert, w_gate, w_up, w_down):
    depth = w_ada.shape[0]
    bp, seq, d = x_prompt.shape
    bs, dseq, _ = x_sample.shape
    past_len = cache_k.shape[2]

    n_c = bp + bs
    n_cp = -(-n_c // SUBLANES) * SUBLANES
    c_all = jnp.concatenate([c_prompt, c_sample, jnp.zeros((n_cp - n_c, d), F32)], axis=0)
    mod_all = _ada_call(c_all, w_ada, b_ada)

    blk = np.arange(LANES) // HEAD_DIM
    bd = jnp.asarray(blk[:, None] == blk[None, :], BF16)
    tri = jnp.asarray(np.arange(LANES)[:, None] <= np.arange(LANES)[None, :], BF16)
    consts = (bd, tri)

    tabs_p = _rope_tables(jnp.arange(seq, dtype=jnp.int32))
    tabs_s = _rope_tables(past_len + jnp.arange(dseq, dtype=jnp.int32))

    xp, xs = x_prompt, x_sample
    st_p, st_s = [], []
    for l in range(depth):
        lw = _layer_weights(l, w_in, q_norm_g, k_norm_g, conv_w, w_out, w_router_group, b_router_group,
                            w_router_expert, b_router_expert, w_gate, w_up, w_down)
        common = (lw, g_norm_mix[l], g_norm_ffn[l], conv_w[l], conv_b[l], ln_conv_g[l], ln_conv_b[l], consts)
        mod_p = mod_all[l, :bp].reshape(bp, 1, 6 * d)
        mod_s = mod_all[l, bp:bp + bs].reshape(bs, 1, 6 * d)
        xp, st = _trunk_layer(xp, mod_p, tabs_p, None, *common,
                              bb=1, rt=512, tq=512, tt=256, mbb=1, mrt=1024)
        st_p.append(st)
        past = (cache_k[l], cache_v[l], cache_idx_k[l], state_conv[l])
        xs, st = _trunk_layer(xs, mod_s, tabs_s, past, *common,
                              bb=8, rt=dseq, tq=dseq, tt=dseq, mbb=16, mrt=dseq)
        st_s.append(st)

    stack = lambda sts, i: jnp.stack([s[i] for s in sts])
    return (xp, xs, stack(st_p, 0), stack(st_p, 1), stack(st_p, 2), stack(st_p, 3),
            stack(st_s, 0), stack(st_s, 1), stack(st_s, 2), stack(st_s, 3))
```
